```python
import jax, jax.numpy as jnp
from jax import lax
import numpy as np

D_MODEL = 1024
BATCH = 16
SEQ = 2048
DEPTH = 1

GRID_W = 64
CTX_LEN = 256
D_MIX = D_MODEL
HEAD_DIM = 64
D_NA = D_MIX // 2
D_RW = D_MIX - D_NA
H_NA = D_NA // HEAD_DIM
H_RW = D_RW // HEAD_DIM
NA_KH = 8
NA_KW = 16
NA_CB = 16
NA_CBW = 2 * NA_CB
RW_LORA_W = 64
RW_LORA_A = 64
RW_LORA_G = 128
N_DIR = 2
D_RW_IN = 3 * D_RW + N_DIR * (RW_LORA_W + RW_LORA_A) + RW_LORA_G
D_IN = 3 * D_NA + D_RW_IN
MOE_GROUPS = 4
MOE_PER_GROUP = 8
N_EXPERTS = MOE_GROUPS * MOE_PER_GROUP
MOE_TOPK = 2
D_EXPERT = 512
ROPE_THETA = 10000.0
NORM_EPS = 1e-6
RW_LN_EPS = 64e-5
NEG = -1e30

kernel_name = 'hymba_natten_rwkv7_hmoe_dit_block'


def rmsnorm(x, g):
    xf = x.astype(jnp.float32)
    y = xf * lax.rsqrt(jnp.mean(xf * xf, axis=-1, keepdims=True) + NORM_EPS)
    return (y * g.astype(jnp.float32)).astype(x.dtype)


def modulate(h, shift, scale):
    return h * (1 + scale) + shift


def axial_rope(t):
    B, T, H, hd = t.shape
    nf = hd // 4
    pos = jnp.arange(T)
    inv = ROPE_THETA ** (-jnp.arange(nf, dtype=jnp.float32) / nf)
    ang = jnp.stack([(pos // GRID_W).astype(jnp.float32)[:, None] * inv,
                     (pos % GRID_W).astype(jnp.float32)[:, None] * inv], axis=1)
    cos = jnp.cos(ang)[None, :, None].astype(t.dtype)
    sin = jnp.sin(ang)[None, :, None].astype(t.dtype)
    t = t.reshape(B, T, H, 2, 2, nf)
    t1, t2 = t[..., 0, :], t[..., 1, :]
    return jnp.stack([t1 * cos - t2 * sin, t2 * cos + t1 * sin], axis=-2).reshape(B, T, H, hd)


def na_heads(p, q_g, k_g):
    B, T, _ = p.shape
    q, k, v = jnp.split(p, 3, axis=-1)
    sh = lambda t: t.reshape(B, T, H_NA, HEAD_DIM)
    q = rmsnorm(sh(q), q_g).transpose(0, 2, 1, 3)
    k = rmsnorm(sh(k), k_g).transpose(0, 2, 1, 3)
    return q, k, sh(v).transpose(0, 2, 1, 3)


def na_latent(q, k, v, kc, vc, rpb):
    B, H, T, hd = q.shape
    rows = T // GRID_W
    kh = min(NA_KH, rows)
    qb = 8 if rows % 8 == 0 else (4 if rows % 4 == 0 else 2)
    bh = min(kh + qb - 1, rows)
    n_rb, n_cb = rows // qb, GRID_W // NA_CB
    row_start = jnp.clip(jnp.arange(rows) - kh // 2, 0, rows - kh)
    col_start = jnp.clip(jnp.arange(GRID_W) - NA_KW // 2, 0, GRID_W - NA_KW)
    band_r = jnp.clip(row_start[::qb], 0, rows - bh)[:, None] + jnp.arange(bh)
    band_c = jnp.clip(col_start[::NA_CB], 0, GRID_W - NA_CBW)[:, None] + jnp.arange(NA_CBW)

    def band(t):
        g = t.reshape(B, H, rows, GRID_W, hd)
        g = jnp.take(g, band_r, axis=2)
        g = jnp.take(g, band_c, axis=4)
        return g.transpose(0, 1, 2, 4, 3, 5, 6).reshape(B, H, n_rb, n_cb, bh * NA_CBW, hd)

    kb, vb = band(k), band(v)
    qg = q.reshape(B, H, n_rb, qb, n_cb, NA_CB, hd).transpose(0, 1, 2, 4, 3, 5, 6)
    qg = qg.reshape(B, H, n_rb, n_cb, qb * NA_CB, hd)

    q_row = jnp.arange(n_rb)[:, None] * qb + jnp.arange(qb)
    q_col = jnp.arange(n_cb)[:, None] * NA_CB + jnp.arange(NA_CB)
    rs = row_start[q_row][:, :, None]
    row_ok = (band_r[:, None, :] >= rs) & (band_r[:, None, :] < rs + kh)
    cs = col_start[q_col][:, :, None]
    col_ok = (band_c[:, None, :] >= cs) & (band_c[:, None, :] < cs + NA_KW)
    ok = (row_ok[:, None, :, None, :, None] & col_ok[None, :, None, :, None, :])
    ok = ok.reshape(n_rb, n_cb, qb * NA_CB, bh * NA_CBW)
    dr = jnp.clip(band_r[:, None, :] - q_row[:, :, None], -(NA_KH - 1), NA_KH - 1) + NA_KH - 1
    dc = jnp.clip(band_c[:, None, :] - q_col[:, :, None], -(NA_KW - 1), NA_KW - 1) + NA_KW - 1
    bias = rpb[:, dr[:, None, :, None, :, None], dc[None, :, None, :, None, :]]
    bias = bias.reshape(H, n_rb, n_cb, qb * NA_CB, bh * NA_CBW).astype(jnp.float32)

    scale = hd ** -0.5
    s_win = jnp.einsum('bhrcqd,bhrckd->bhrcqk', qg, kb).astype(jnp.float32) * scale + bias[None]
    s_win = jnp.where(ok, s_win, NEG)
    s_ctx = jnp.einsum('bhrcqd,bhkd->bhrcqk', qg, kc).astype(jnp.float32) * scale
    p = jax.nn.softmax(jnp.concatenate([s_win, s_ctx], axis=-1), axis=-1).astype(v.dtype)
    n_win = bh * NA_CBW
    o = (jnp.einsum('bhrcqk,bhrckd->bhrcqd', p[..., :n_win], vb)
         + jnp.einsum('bhrcqk,bhkd->bhrcqd', p[..., n_win:], vc))
    o = o.reshape(B, H, n_rb, n_cb, qb, NA_CB, hd).transpose(0, 2, 4, 3, 5, 1, 6)
    return o.reshape(B, T, H * hd)


def na_context(qc, kc, vc):
    B, H, T, hd = qc.shape
    s = jnp.einsum('bhqd,bhkd->bhqk', qc, kc).astype(jnp.float32) * hd ** -0.5
    p = jax.nn.softmax(s, axis=-1).astype(vc.dtype)
    return jnp.einsum('bhqk,bhkd->bqhd', p, vc).reshape(B, T, H * hd)


def centred_shift(p, mu_prev, mu_next):
    prev = jnp.pad(p, ((0, 0), (1, 0), (0, 0)))[:, :-1]
    nxt = jnp.pad(p, ((0, 0), (0, 1), (0, 0)))[:, 1:]
    return p + mu_prev * (prev - p) + mu_next * (nxt - p)


def rwkv_streams(p, mu_prev, mu_next, w0, w_up, a0, a_up, g_up, k_k, k_a, rope):
    B, T, _ = p.shape
    p = centred_shift(p, mu_prev, mu_next)
    o1, o2, o3 = D_RW, 2 * D_RW, 3 * D_RW
    o4 = o3 + N_DIR * RW_LORA_W
    o5 = o4 + N_DIR * RW_LORA_A
    r, k, v, wd, ad, gd = jnp.split(p, [o1, o2, o3, o4, o5], axis=-1)
    heads = lambda t: t.reshape(B, T, H_RW, HEAD_DIM)
    r, k, v = heads(r), heads(k), heads(v)
    if rope:
        r, k = axial_rope(r), axial_rope(k)
    wd = wd.reshape(B, T, N_DIR, RW_LORA_W)
    ad = ad.reshape(B, T, N_DIR, RW_LORA_A)
    w_pre = (w0 + jnp.einsum('btnr,nrc->btnc', jnp.tanh(wd), w_up)).astype(jnp.float32)
    decay = jnp.exp(-jnp.exp(-jax.nn.softplus(-w_pre) - 0.5)).reshape(B, T, N_DIR, H_RW, HEAD_DIM)
    a = jax.nn.sigmoid((a0 + jnp.einsum('btnr,nrc->btnc', ad, a_up)).astype(jnp.float32))
    a = a.reshape(B, T, N_DIR, H_RW, HEAD_DIM)
    g = jax.nn.sigmoid(gd) @ g_up
    rf, kf, vf = r.astype(jnp.float32), k.astype(jnp.float32), v.astype(jnp.float32)
    kk = kf * k_k.reshape(H_RW, HEAD_DIM)
    kk = kk / jnp.maximum(jnp.sqrt(jnp.sum(kk * kk, axis=-1, keepdims=True)), 1e-12)
    akk = kk[:, :, None] * a
    kw = kf[:, :, None] * (1 + (a - 1) * k_a.reshape(H_RW, HEAD_DIM))
    return rf, kf, vf, kk, decay, akk, kw, g


def rwkv_scan(S0, r, decay, kk, akk, kw, v, reverse):
    tm = lambda t: jnp.swapaxes(t, 0, 1)

    def step(S, inp):
        r_t, w_t, kk_t, akk_t, k_t, v_t = inp
        S = (S * w_t[:, :, None, :]
             - jnp.einsum('bhvk,bhk->bhv', S, kk_t)[..., None] * akk_t[:, :, None, :]
             + v_t[..., None] * k_t[:, :, None, :])
        return S, jnp.einsum('bhvk,bhk->bhv', S, r_t)

    S, o = lax.scan(step, S0, tuple(map(tm, (r, decay, kk, akk, kw, v))), reverse=reverse)
    return S, jnp.swapaxes(o, 0, 1)


def rwkv_finish(o, r, k, v, g, r_k, ln_g, ln_b):
    B, T = o.shape[:2]
    mu = jnp.mean(o, axis=-1, keepdims=True)
    var = jnp.mean(jnp.square(o - mu), axis=-1, keepdims=True)
    y = ((o - mu) * lax.rsqrt(var + RW_LN_EPS)).reshape(B, T, D_RW) * ln_g + ln_b
    bonus = jnp.sum(r * k * r_k, axis=-1, keepdims=True) * v
    return ((y + bonus.reshape(B, T, D_RW)) * g).astype(g.dtype)


def token_mixer(hx, hc, w_in, na_q_g, na_k_g, na_rpb, rw_mu_prev, rw_mu_next, rw_w0, rw_w_up, rw_a0,
                rw_a_up, rw_g_up, rw_k_k, rw_k_a, rw_r_k, rw_ln_g, rw_ln_b, w_out, need_ctx):
    px, pc = hx @ w_in, hc @ w_in
    qx, kx, vx = na_heads(px[..., :3 * D_NA], na_q_g, na_k_g)
    qc, kc, vc = na_heads(pc[..., :3 * D_NA], na_q_g, na_k_g)
    na_x = na_latent(qx, kx, vx, kc, vc, na_rpb)

    rw_args = (rw_mu_prev, rw_mu_next, rw_w0, rw_w_up, rw_a0, rw_a_up, rw_g_up, rw_k_k, rw_k_a)
    rl, kl, vl, kkl, decl, akkl, kwl, gl = rwkv_streams(px[..., 3 * D_NA:], *rw_args, rope=True)
    rc, krc, vrc, kkc, decc, akkc, kwc, gc = rwkv_streams(pc[..., 3 * D_NA:], *rw_args, rope=False)
    B = hx.shape[0]
    o_lat, o_ctx = [], []
    for d in range(N_DIR):
        S0 = jnp.zeros((B, H_RW, HEAD_DIM, HEAD_DIM), jnp.float32)
        S_c, oc = rwkv_scan(S0, rc, decc[:, :, d], kkc, akkc[:, :, d], kwc[:, :, d], vrc, d == 1)
        _, ol = rwkv_scan(S_c, rl, decl[:, :, d], kkl, akkl[:, :, d], kwl[:, :, d], vl, d == 1)
        o_lat.append(ol)
        o_ctx.append(oc)
    rw_x = rwkv_finish(o_lat[0] + o_lat[1], rl, kl, vl, gl, rw_r_k, rw_ln_g, rw_ln_b)
    yx = jnp.concatenate([na_x, rw_x], axis=-1) @ w_out
    if not need_ctx:
        return yx, None
    rw_c = rwkv_finish(o_ctx[0] + o_ctx[1], rc, krc, vrc, gc, rw_r_k, rw_ln_g, rw_ln_b)
    yc = jnp.concatenate([na_context(qc, kc, vc), rw_c], axis=-1) @ w_out
    return yx, yc


def hier_moe(h, wg, bg, we, be, w1, w3, w2):
    B, T, D = h.shape
    hf = h.reshape(B * T, D)
    g_logits = (hf @ wg + bg).astype(jnp.float32)
    g_prob = jax.nn.softmax(g_logits, axis=-1)
    g_sel = jnp.argmax(g_logits, axis=-1)
    p_group = jnp.take_along_axis(g_prob, g_sel[:, None], axis=-1)
    e_all = jnp.einsum('nd,gde->nge', hf, we) + be
    e_logits = jnp.take_along_axis(e_all, g_sel[:, None, None], axis=1)[:, 0].astype(jnp.float32)
    top_v, top_i = lax.top_k(e_logits, MOE_TOPK)
    top_w = jax.nn.softmax(top_v, axis=-1) * p_group
    expert_id = g_sel[:, None] * MOE_PER_GROUP + top_i
    combine = jnp.sum(jax.nn.one_hot(expert_id, N_EXPERTS, dtype=jnp.float32) * top_w[..., None], axis=1)
    combine = combine.astype(h.dtype)
    out = jnp.zeros_like(hf)
    for e in range(N_EXPERTS):
        he = jax.nn.silu(hf @ w1[e]) * (hf @ w3[e])
        out = out + combine[:, e:e + 1] * (he @ w2[e])
    return out.reshape(B, T, D)


def setup_inputs(seed: int = 0) -> dict:
    key = jax.random.key(seed)
    ks = iter(jax.random.split(key, 40))
    f32 = jnp.float32
    nrm = lambda shape, s: jax.random.normal(next(ks), shape, f32) * s
    L, D = DEPTH, D_MODEL
    w0_base = jnp.tile(-6.0 + 5.0 * jnp.linspace(0.0, 1.0, HEAD_DIM, dtype=f32) ** 0.9, H_RW)
    return {
        'x': nrm((BATCH, SEQ, D), 1.0),
        'c': nrm((BATCH, D), 1.0),
        'ctx': nrm((BATCH, CTX_LEN, D), 1.0),
        'c_ctx': nrm((D,), 1.0),
        'w_mod': nrm((L, D, 6 * D), 0.5 * D ** -0.5),
        'b_mod': nrm((L, 6 * D), 0.02),
        'norm1_g': 1.0 + nrm((L, D), 0.02),
        'norm2_g': 1.0 + nrm((L, D), 0.02),
        'w_in': nrm((L, D, D_IN), D ** -0.5),
        'na_q_g': 1.0 + nrm((L, HEAD_DIM), 0.02),
        'na_k_g': 1.0 + nrm((L, HEAD_DIM), 0.02),
        'na_rpb': nrm((L, H_NA, 2 * NA_KH - 1, 2 * NA_KW - 1), 0.2),
        'rw_mu_prev': 0.5 * jax.random.uniform(next(ks), (L, D_RW_IN), f32),
        'rw_mu_next': 0.5 * jax.random.uniform(next(ks), (L, D_RW_IN), f32),
        'rw_w0': w0_base + nrm((L, N_DIR, D_RW), 0.1),
        'rw_w_up': nrm((L, N_DIR, RW_LORA_W, D_RW), 0.1),
        'rw_a0': nrm((L, N_DIR, D_RW), 0.1),
        'rw_a_up': nrm((L, N_DIR, RW_LORA_A, D_RW), RW_LORA_A ** -0.5),
        'rw_g_up': nrm((L, RW_LORA_G, D_RW), RW_LORA_G ** -0.5),
        'rw_k_k': 0.85 + nrm((L, D_RW), 0.02),
        'rw_k_a': 1.0 + nrm((L, D_RW), 0.02),
        'rw_r_k': nrm((L, H_RW, HEAD_DIM), 0.1),
        'rw_ln_g': 1.0 + nrm((L, D_RW), 0.02),
        'rw_ln_b': nrm((L, D_RW), 0.02),
        'w_out': nrm((L, D_MIX, D), D_MIX ** -0.5),
        'moe_wg': nrm((L, D, MOE_GROUPS), D ** -0.5),
        'moe_bg': nrm((L, MOE_GROUPS), 0.01),
        'moe_we': nrm((L, MOE_GROUPS, D, MOE_PER_GROUP), D ** -0.5),
        'moe_be': nrm((L, MOE_GROUPS, MOE_PER_GROUP), 0.01),
        'moe_w1': nrm((L, N_EXPERTS, D, D_EXPERT), D ** -0.5),
        'moe_w3': nrm((L, N_EXPERTS, D, D_EXPERT), D ** -0.5),
        'moe_w2': nrm((L, N_EXPERTS, D_EXPERT, D), D_EXPERT ** -0.5),
    }


def reference(x, c, ctx, c_ctx, w_mod, b_mod, norm1_g, norm2_g, w_in, na_q_g, na_k_g, na_rpb,
              rw_mu_prev, rw_mu_next, rw_w0, rw_w_up, rw_a0, rw_a_up, rw_g_up, rw_k_k, rw_k_a, rw_r_k,
              rw_ln_g, rw_ln_b, w_out, moe_wg, moe_bg, moe_we, moe_be, moe_w1, moe_w3, moe_w2):
    for l in range(DEPTH):
        last = l == DEPTH - 1
        mod_x = jnp.split((jax.nn.silu(c) @ w_mod[l] + b_mod[l])[:, None, :], 6, axis=-1)
        mod_c = jnp.split(jax.nn.silu(c_ctx) @ w_mod[l] + b_mod[l], 6, axis=-1)
        hx = modulate(rmsnorm(x, norm1_g[l]), mod_x[0], mod_x[1])
        hc = modulate(rmsnorm(ctx, norm1_g[l]), mod_c[0], mod_c[1])
        yx, yc = token_mixer(hx, hc, w_in[l], na_q_g[l], na_k_g[l], na_rpb[l], rw_mu_prev[l], rw_mu_next[l],
                             rw_w0[l], rw_w_up[l], rw_a0[l], rw_a_up[l], rw_g_up[l], rw_k_k[l], rw_k_a[l],
                             rw_r_k[l], rw_ln_g[l], rw_ln_b[l], w_out[l], not last)
        x = x + mod_x[2] * yx
        hx2 = modulate(rmsnorm(x, norm2_g[l]), mod_x[3], mod_x[4])
        x = x + mod_x[5] * hier_moe(hx2, moe_wg[l], moe_bg[l], moe_we[l], moe_be[l], moe_w1[l], moe_w3[l], moe_w2[l])
        if not last:
            ctx = ctx + mod_c[2] * yc
            hc2 = modulate(rmsnorm(ctx, norm2_g[l]), mod_c[3], mod_c[4])
            ctx = ctx + mod_c[5] * hier_moe(hc2, moe_wg[l], moe_bg[l], moe_we[l], moe_be[l], moe_w1[l], moe_w3[l], moe_w2[l])
    return x
```

```python
import functools
import math

import jax
import jax.numpy as jnp
import numpy as np
from jax import lax
from jax.experimental import pallas as pl
from jax.experimental.pallas import tpu as pltpu

D_MODEL = 1024
BATCH = 16
SEQ = 2048
GRID_W = 64
GRID_H = SEQ // GRID_W
CTX_LEN = 256
T_ALL = CTX_LEN + SEQ
HEAD_DIM = 64
D_NA = 512
D_RW = 512
H_NA = D_NA // HEAD_DIM
H_RW = D_RW // HEAD_DIM
NA_KH = 8
NA_KW = 16
LORA_W = 64
LORA_A = 64
LORA_G = 128
D_RW_IN = 3 * D_RW + 2 * (LORA_W + LORA_A) + LORA_G
D_IN = 3 * D_NA + D_RW_IN
MOE_GROUPS = 4
MOE_PER_GROUP = 8
N_EXPERTS = MOE_GROUPS * MOE_PER_GROUP
D_EXPERT = 512
ROPE_THETA = 10000.0
NORM_EPS = 1e-6
RW_LN_EPS = 64e-5
NEG = -1e30
DECAY_SCALE = math.exp(-0.5)

LANES = 128
TOK_TILE = 256
N_TILES = T_ALL // TOK_TILE
SCAN_CHUNK = 16
MOE_TILE = 256
N_TOK = BATCH * SEQ
N_SLOTS = 2 * N_TOK + N_EXPERTS * MOE_TILE
N_MOE_TILES = N_SLOTS // MOE_TILE
LOGIT_ROWS = 8 + N_EXPERTS
VMEM_LIMIT = 48 * 1024 * 1024

_HI = lax.Precision.HIGHEST


def _sigmoid(x):
    return 1.0 / (1.0 + jnp.exp(-x))


def _silu(x):
    return x * _sigmoid(x)


def _rmsnorm_rows(xf, g):
    return xf * lax.rsqrt(jnp.mean(xf * xf, axis=-1, keepdims=True) + NORM_EPS) * g


def _group_sum(x, bd):
    hi = x.astype(jnp.bfloat16)
    lo = (x - hi.astype(jnp.float32)).astype(jnp.bfloat16)
    return (jnp.dot(hi, bd, preferred_element_type=jnp.float32)
            + jnp.dot(lo, bd, preferred_element_type=jnp.float32))


def _mod_kernel(c_ref, w_ref, b_ref, o_ref):
    o_ref[...] = jnp.dot(_silu(c_ref[...]), w_ref[...], precision=_HI,
                         preferred_element_type=jnp.float32) + b_ref[...]


def _mod_call(cs, w_mod, b_mod):
    rows = cs.shape[0]
    n = w_mod.shape[1]
    blk = 1024
    return pl.pallas_call(
        _mod_kernel,
        out_shape=jax.ShapeDtypeStruct((rows, n), jnp.float32),
        grid=(n // blk,),
        in_specs=[pl.BlockSpec((rows, D_MODEL), lambda j: (0, 0)),
                  pl.BlockSpec((D_MODEL, blk), lambda j: (0, j)),
                  pl.BlockSpec((1, blk), lambda j: (0, j))],
        out_specs=pl.BlockSpec((rows, blk), lambda j: (0, j)),
        name="mod",
    )(cs, w_mod, b_mod)


def _inproj_kernel(x_ref, ctx_ref, modx_ref, modc_ref, g_ref, w_ref, na_ref, rw_ref):
    is_ctx = pl.program_id(1) == 0
    xin = jnp.where(is_ctx, ctx_ref[0], x_ref[0])
    mod = jnp.where(is_ctx, modc_ref[0], modx_ref[0])
    h = _rmsnorm_rows(xin, g_ref[...]) * (1.0 + mod[1:2, :]) + mod[0:1, :]
    p = jnp.dot(h.astype(jnp.bfloat16), w_ref[...], preferred_element_type=jnp.float32)
    na_ref[0] = p[:, :3 * D_NA]
    rw_ref[0] = p[:, 3 * D_NA:]


def _inproj_call(x, ctx, mod3, norm_g, w_in_bf16):
    return pl.pallas_call(
        _inproj_kernel,
        out_shape=(jax.ShapeDtypeStruct((BATCH, T_ALL, 3 * D_NA), jnp.float32),
                   jax.ShapeDtypeStruct((BATCH, T_ALL, D_RW_IN), jnp.float32)),
        grid=(BATCH, N_TILES),
        in_specs=[pl.BlockSpec((1, TOK_TILE, D_MODEL), lambda b, j: (b, jnp.maximum(j - 1, 0), 0)),
                  pl.BlockSpec((1, TOK_TILE, D_MODEL), lambda b, j: (b, 0, 0)),
                  pl.BlockSpec((1, 6, D_MODEL), lambda b, j: (b, 0, 0)),
                  pl.BlockSpec((1, 6, D_MODEL), lambda b, j: (BATCH, 0, 0)),
                  pl.BlockSpec((1, D_MODEL), lambda b, j: (0, 0)),
                  pl.BlockSpec((D_MODEL, D_IN), lambda b, j: (0, 0))],
        out_specs=(pl.BlockSpec((1, TOK_TILE, 3 * D_NA), lambda b, j: (b, j, 0)),
                   pl.BlockSpec((1, TOK_TILE, D_RW_IN), lambda b, j: (b, j, 0))),
        compiler_params=pltpu.CompilerParams(vmem_limit_bytes=VMEM_LIMIT),
        name="inproj",
    )(x, ctx, mod3, mod3, norm_g, w_in_bf16)


def _na_kernel(q_ref, k_ref, v_ref, bias_ref, qg_ref, kg_ref, o_ref, qs, ks, vs):
    for h in range(2):
        hs = slice(h * HEAD_DIM, (h + 1) * HEAD_DIM)
        qs[...] = (_rmsnorm_rows(q_ref[0, CTX_LEN:, hs], qg_ref[...]) * (HEAD_DIM ** -0.5)).astype(jnp.bfloat16)
        ks[...] = _rmsnorm_rows(k_ref[0, :, hs], kg_ref[...]).astype(jnp.bfloat16)
        vs[...] = v_ref[0, :, hs].astype(jnp.bfloat16)
        kc = ks[0:CTX_LEN, :]
        vc = vs[0:CTX_LEN, :]

        def row_body(qr, carry):
            rs = jnp.clip(qr - NA_KH // 2, 0, GRID_H - NA_KH)
            q0 = pl.multiple_of(qr * GRID_W, GRID_W)
            k0 = pl.multiple_of(CTX_LEN + rs * GRID_W, GRID_W)
            qb = qs[pl.ds(q0, GRID_W), :]
            kw = ks[pl.ds(k0, NA_KH * GRID_W), :]
            vw = vs[pl.ds(k0, NA_KH * GRID_W), :]
            nt = (((1,), (1,)), ((), ()))
            s_win = lax.dot_general(qb, kw, nt, preferred_element_type=jnp.float32) + bias_ref[h, rs - qr + NA_KH - 1]
            s_ctx = lax.dot_general(qb, kc, nt, preferred_element_type=jnp.float32)
            m = jnp.maximum(jnp.max(s_win, axis=-1, keepdims=True), jnp.max(s_ctx, axis=-1, keepdims=True))
            e_win = jnp.exp(s_win - m)
            e_ctx = jnp.exp(s_ctx - m)
            den = jnp.sum(e_win, axis=-1, keepdims=True) + jnp.sum(e_ctx, axis=-1, keepdims=True)
            o = (jnp.dot(e_win.astype(jnp.bfloat16), vw, preferred_element_type=jnp.float32)
                 + jnp.dot(e_ctx.astype(jnp.bfloat16), vc, preferred_element_type=jnp.float32))
            o_ref[0, pl.ds(q0, GRID_W), hs] = o / den
            return carry

        lax.fori_loop(0, GRID_H, row_body, 0)


def _na_call(p_na, bias8, q_g, k_g):
    n_hp = D_NA // LANES
    return pl.pallas_call(
        _na_kernel,
        out_shape=jax.ShapeDtypeStruct((BATCH, SEQ, D_NA), jnp.float32),
        grid=(n_hp, BATCH),
        in_specs=[pl.BlockSpec((1, T_ALL, LANES), lambda hp, b: (b, 0, hp)),
                  pl.BlockSpec((1, T_ALL, LANES), lambda hp, b: (b, 0, n_hp + hp)),
                  pl.BlockSpec((1, T_ALL, LANES), lambda hp, b: (b, 0, 2 * n_hp + hp)),
                  pl.BlockSpec((2, NA_KH, GRID_W, NA_KH * GRID_W), lambda hp, b: (hp, 0, 0, 0)),
                  pl.BlockSpec((1, HEAD_DIM), lambda hp, b: (0, 0)),
                  pl.BlockSpec((1, HEAD_DIM), lambda hp, b: (0, 0))],
        out_specs=pl.BlockSpec((1, SEQ, LANES), lambda hp, b: (b, 0, hp)),
        scratch_shapes=[pltpu.VMEM((SEQ, HEAD_DIM), jnp.bfloat16),
                        pltpu.VMEM((T_ALL, HEAD_DIM), jnp.bfloat16),
                        pltpu.VMEM((T_ALL, HEAD_DIM), jnp.bfloat16)],
        compiler_params=pltpu.CompilerParams(vmem_limit_bytes=VMEM_LIMIT),
        name="na",
    )(p_na, p_na, p_na, bias8, q_g, k_g)


def _na_bias_table(rpb):
    qc = np.arange(GRID_W)
    cs = np.clip(qc - NA_KW // 2, 0, GRID_W - NA_KW)
    kc = np.arange(GRID_W)
    col_ok = (kc[None, :] >= cs[:, None]) & (kc[None, :] < cs[:, None] + NA_KW)
    dc = np.clip(kc[None, :] - qc[:, None], -(NA_KW - 1), NA_KW - 1) + NA_KW - 1
    tz = jnp.where(col_ok[None, None], rpb[:, :, dc], NEG)
    slabs = []
    for p in range(NA_KH):
        slabs.append(jnp.concatenate([tz[:, p + j] for j in range(NA_KH)], axis=-1))
    return jnp.stack(slabs, axis=1).astype(jnp.float32)


def _swap16(x):
    lane = lax.broadcasted_iota(jnp.int32, x.shape, 1)
    return jnp.where((lane & 16) == 0, pltpu.roll(x, LANES - 16, axis=1), pltpu.roll(x, 16, axis=1))


def _rope(x, cos, sin):
    blocks = []
    for i in range(x.shape[1] // LANES):
        xb = x[:, i * LANES:(i + 1) * LANES]
        blocks.append(xb * cos + _swap16(xb) * sin)
    return jnp.concatenate(blocks, axis=1)


def _prep_kernel(p_ref, pv_ref, nx_ref, cos_ref, sin_ref, mup_ref, mun_ref, w0_ref, wup_ref, a0_ref, aup_ref,
                 gup_ref, kk_ref, ka_ref, rk_ref, bd_ref,
                 r_out, kkn_out, v_out, dec_out, akk_out, kw_out, g_out, bonus_out):
    j = pl.program_id(1)
    p = p_ref[0]
    row = lax.broadcasted_iota(jnp.int32, p.shape, 0)
    prev_row = jnp.where((j == 0) | (j == 1), 0.0, pv_ref[0, 7:8, :])
    next_row = jnp.where((j == 0) | (j == N_TILES - 1), 0.0, nx_ref[0, 0:1, :])
    prev = jnp.where(row == 0, prev_row, pltpu.roll(p, 1, axis=0))
    nxt = jnp.where(row == TOK_TILE - 1, next_row, pltpu.roll(p, TOK_TILE - 1, axis=0))
    ps = p + mup_ref[...] * (prev - p) + mun_ref[...] * (nxt - p)

    cos = cos_ref[...]
    sin = sin_ref[...]
    r = _rope(ps[:, 0:D_RW], cos, sin)
    k = _rope(ps[:, D_RW:2 * D_RW], cos, sin)
    v = ps[:, 2 * D_RW:3 * D_RW]
    o3 = 3 * D_RW
    wd = ps[:, o3:o3 + 2 * LORA_W]
    ad = ps[:, o3 + 2 * LORA_W:o3 + 2 * LORA_W + 2 * LORA_A]
    gd = ps[:, o3 + 2 * LORA_W + 2 * LORA_A:]

    w_pre = w0_ref[...] + jnp.dot(jnp.tanh(wd).astype(jnp.bfloat16), wup_ref[...], preferred_element_type=jnp.float32)
    decay = jnp.exp(-DECAY_SCALE * _sigmoid(w_pre))
    a = _sigmoid(a0_ref[...] + jnp.dot(ad.astype(jnp.bfloat16), aup_ref[...], preferred_element_type=jnp.float32))
    g = jnp.dot(_sigmoid(gd).astype(jnp.bfloat16), gup_ref[...], preferred_element_type=jnp.float32)

    bd = bd_ref[...]
    kk = k * kk_ref[...]
    kk = kk / jnp.maximum(jnp.sqrt(_group_sum(kk * kk, bd)), 1e-12)
    bonus = _group_sum(r * k * rk_ref[...], bd) * v
    ka = ka_ref[...]

    r_out[0] = r
    kkn_out[0] = kk
    v_out[0] = v
    g_out[0] = g
    bonus_out[0] = bonus
    for d in range(2):
        a_d = a[:, d * D_RW:(d + 1) * D_RW]
        dec_out[0, :, d * D_RW:(d + 1) * D_RW] = decay[:, d * D_RW:(d + 1) * D_RW]
        akk_out[0, :, d * D_RW:(d + 1) * D_RW] = kk * a_d
        kw_out[0, :, d * D_RW:(d + 1) * D_RW] = k * (1.0 + (a_d - 1.0) * ka)


def _prep_call(p_rw, cos_t, sin_t, mu_prev, mu_next, w0, wup2, a0, aup2, gup, k_k, k_a, r_k, bd):
    sub = TOK_TILE // 8
    n_sub = T_ALL // 8
    full = lambda shape: pl.BlockSpec(shape, lambda b, j: (0,) * len(shape))
    tok = lambda width: pl.BlockSpec((1, TOK_TILE, width), lambda b, j: (b, j, 0))
    one = jax.ShapeDtypeStruct((BATCH, T_ALL, D_RW), jnp.float32)
    two = jax.ShapeDtypeStruct((BATCH, T_ALL, 2 * D_RW), jnp.float32)
    return pl.pallas_call(
        _prep_kernel,
        out_shape=(one, one, one, two, two, two, one, one),
        grid=(BATCH, N_TILES),
        in_specs=[tok(D_RW_IN),
                  pl.BlockSpec((1, 8, D_RW_IN), lambda b, j: (b, jnp.maximum(j * sub - 1, 0), 0)),
                  pl.BlockSpec((1, 8, D_RW_IN), lambda b, j: (b, jnp.minimum((j + 1) * sub, n_sub - 1), 0)),
                  pl.BlockSpec((TOK_TILE, LANES), lambda b, j: (j, 0)),
                  pl.BlockSpec((TOK_TILE, LANES), lambda b, j: (j, 0)),
                  full((1, D_RW_IN)), full((1, D_RW_IN)),
                  full((1, 2 * D_RW)), full((2 * LORA_W, 2 * D_RW)),
                  full((1, 2 * D_RW)), full((2 * LORA_A, 2 * D_RW)),
                  full((LORA_G, D_RW)),
                  full((1, D_RW)), full((1, D_RW)), full((1, D_RW)),
                  full((D_RW, D_RW))],
        out_specs=(tok(D_RW), tok(D_RW), tok(D_RW), tok(2 * D_RW), tok(2 * D_RW), tok(2 * D_RW), tok(D_RW), tok(D_RW)),
        compiler_params=pltpu.CompilerParams(vmem_limit_bytes=VMEM_LIMIT),
        name="prep",
    )(p_rw, p_rw, p_rw, cos_t, sin_t, mu_prev, mu_next, w0, wup2, a0, aup2, gup, k_k, k_a, r_k, bd)


def _rope_tables():
    nf = HEAD_DIM // 4
    pos = np.arange(SEQ)
    inv = ROPE_THETA ** (-np.arange(nf, dtype=np.float32) / nf)
    lane = np.arange(LANES) % HEAD_DIM
    half, pair, f = lane // 32, (lane % 32) // 16, lane % 16
    coord = np.where(half[None, :] == 0, (pos // GRID_W)[:, None], (pos % GRID_W)[:, None]).astype(np.float32)
    ang = coord * inv[f][None, :].astype(np.float32)
    cos = np.cos(ang).astype(np.float32)
    sin = np.sin(ang).astype(np.float32) * np.where(pair == 0, -1.0, 1.0)[None, :].astype(np.float32)
    cos = np.concatenate([np.ones((CTX_LEN, LANES), np.float32), cos], axis=0)
    sin = np.concatenate([np.zeros((CTX_LEN, LANES), np.float32), sin], axis=0)
    return jnp.asarray(cos), jnp.asarray(sin)


def _scan_kernel(r_ref, kk_ref, v_ref, w_ref, akk_ref, kw_ref, o_ref, s_ref):
    d = pl.program_id(0)
    i = pl.program_id(1)

    @pl.when(i == 0)
    def _():
        s_ref[...] = jnp.zeros_like(s_ref)

    k_unroll = 8

    def step(s, carry):
        t = jnp.where(d == 0, s, SCAN_CHUNK - 1 - s)
        vv = v_ref[t]

        def sk_body(kb, sk):
            for u in range(k_unroll):
                k = kb * k_unroll + u
                sk = sk + s_ref[k] * kk_ref[t, pl.ds(k, 1), :]
            return sk

        sk = lax.fori_loop(0, HEAD_DIM // k_unroll, sk_body, jnp.zeros((HEAD_DIM, LANES), jnp.float32))

        def upd_body(kb, o):
            for u in range(k_unroll):
                k = kb * k_unroll + u
                s_new = (s_ref[k] * w_ref[0, t, pl.ds(k, 1), :] - sk * akk_ref[0, t, pl.ds(k, 1), :]
                         + vv * kw_ref[0, t, pl.ds(k, 1), :])
                s_ref[k] = s_new
                o = o + s_new * r_ref[t, pl.ds(k, 1), :]
            return o

        o_ref[0, t] = lax.fori_loop(0, HEAD_DIM // k_unroll, upd_body, jnp.zeros((HEAD_DIM, LANES), jnp.float32))
        return carry

    lax.fori_loop(0, SCAN_CHUNK, step, 0)


def _scan_call(r_t, kk_t, v_t, w_t, akk_t, kw_t):
    nc = CTX_LEN // SCAN_CHUNK
    nl = SEQ // SCAN_CHUNK

    def blk(d, i):
        rev = jnp.where(i < nc, nc - 1 - i, 2 * nc + nl - 1 - i)
        return jnp.where(d == 0, i, rev)

    def oblk(d, i):
        fwd = jnp.maximum(i - nc, 0)
        rev = jnp.where(i < nc, nl - 1, nc + nl - 1 - i)
        return jnp.where(d == 0, fwd, rev)

    shared = pl.BlockSpec((SCAN_CHUNK, HEAD_DIM, LANES), lambda d, i: (blk(d, i), 0, 0))
    perdir = pl.BlockSpec((1, SCAN_CHUNK, HEAD_DIM, LANES), lambda d, i: (d, blk(d, i), 0, 0))
    return pl.pallas_call(
        _scan_kernel,
        out_shape=jax.ShapeDtypeStruct((2, SEQ, HEAD_DIM, LANES), jnp.float32),
        grid=(2, nc + nl),
        in_specs=[shared, shared, shared, perdir, perdir, perdir],
        out_specs=pl.BlockSpec((1, SCAN_CHUNK, HEAD_DIM, LANES), lambda d, i: (d, oblk(d, i), 0, 0)),
        scratch_shapes=[pltpu.VMEM((HEAD_DIM, HEAD_DIM, LANES), jnp.float32)],
        compiler_params=pltpu.CompilerParams(dimension_semantics=("arbitrary", "arbitrary"),
                                             vmem_limit_bytes=VMEM_LIMIT),
        name="scan",
    )(r_t, kk_t, v_t, w_t, akk_t, kw_t)


def _to_scan_layout(a):
    n = a.shape[-1] // D_RW
    a = a.reshape(BATCH, T_ALL, n, H_RW, HEAD_DIM)
    return a.transpose(2, 1, 4, 0, 3).reshape(n, T_ALL, HEAD_DIM, LANES)


def _from_scan_layout(o):
    return o.reshape(SEQ, HEAD_DIM, BATCH, H_RW).transpose(2, 0, 3, 1).reshape(BATCH, SEQ, D_RW)


def _finish_kernel(na_ref, o_ref, bonus_ref, g_ref, x_ref, mod_ref, lng_ref, lnb_ref, wout_ref, n2g_ref, bd_ref,
                   wr_ref, br_ref, x1_out, h2_out, eid_out, wt_out):
    bd = bd_ref[...]
    o = o_ref[0]
    mu = _group_sum(o, bd) * (1.0 / HEAD_DIM)
    oc = o - mu
    var = _group_sum(oc * oc, bd) * (1.0 / HEAD_DIM)
    y = oc * lax.rsqrt(var + RW_LN_EPS) * lng_ref[...] + lnb_ref[...]
    rw = (y + bonus_ref[0]) * g_ref[0]
    mix = jnp.concatenate([na_ref[0], rw], axis=-1).astype(jnp.bfloat16)
    yx = jnp.dot(mix, wout_ref[...], preferred_element_type=jnp.float32)
    mod = mod_ref[0]
    x1 = x_ref[0] + mod[2:3, :] * yx
    x1_out[0] = x1
    h2 = _rmsnorm_rows(x1, n2g_ref[...]) * (1.0 + mod[4:5, :]) + mod[3:4, :]
    h2_out[0] = h2.astype(jnp.bfloat16)

    lg = lax.dot_general(wr_ref[...], h2, (((1,), (1,)), ((), ())), precision=_HI,
                         preferred_element_type=jnp.float32) + br_ref[...]
    gl = lg[0:MOE_GROUPS, :]
    grow = lax.broadcasted_iota(jnp.int32, gl.shape, 0)
    gmax = jnp.max(gl, axis=0, keepdims=True)
    g_sel = jnp.min(jnp.where(gl == gmax, grow, MOE_GROUPS), axis=0, keepdims=True)
    p_group = 1.0 / jnp.sum(jnp.exp(gl - gmax), axis=0, keepdims=True)
    el = jnp.zeros((MOE_PER_GROUP, TOK_TILE), jnp.float32)
    for gi in range(MOE_GROUPS):
        el = jnp.where(g_sel == gi, lg[8 + gi * MOE_PER_GROUP:8 + (gi + 1) * MOE_PER_GROUP, :], el)
    erow = lax.broadcasted_iota(jnp.int32, el.shape, 0)
    m1 = jnp.max(el, axis=0, keepdims=True)
    i1 = jnp.min(jnp.where(el == m1, erow, MOE_PER_GROUP), axis=0, keepdims=True)
    el2 = jnp.where(erow == i1, -jnp.inf, el)
    m2 = jnp.max(el2, axis=0, keepdims=True)
    i2 = jnp.min(jnp.where(el2 == m2, erow, MOE_PER_GROUP), axis=0, keepdims=True)
    e21 = jnp.exp(m2 - m1)
    w1 = p_group / (1.0 + e21)
    w2 = p_group * e21 / (1.0 + e21)
    eid_out[0:1, :] = g_sel * MOE_PER_GROUP + i1
    eid_out[1:2, :] = g_sel * MOE_PER_GROUP + i2
    wt_out[0:1, :] = w1
    wt_out[1:2, :] = w2


def _finish_call(na_x, o_rw, bonus, g, x, mod3, ln_g, ln_b, w_out_bf16, norm2_g, bd, wr_t, br):
    n_lat = SEQ // TOK_TILE
    full = lambda shape: pl.BlockSpec(shape, lambda b, j: (0,) * len(shape))
    lat = lambda width: pl.BlockSpec((1, TOK_TILE, width), lambda b, j: (b, j, 0))
    allt = lambda width: pl.BlockSpec((1, TOK_TILE, width), lambda b, j: (b, j + 1, 0))
    return pl.pallas_call(
        _finish_kernel,
        out_shape=(jax.ShapeDtypeStruct((BATCH, SEQ, D_MODEL), jnp.float32),
                   jax.ShapeDtypeStruct((BATCH, SEQ, D_MODEL), jnp.bfloat16),
                   jax.ShapeDtypeStruct((2, N_TOK), jnp.int32),
                   jax.ShapeDtypeStruct((2, N_TOK), jnp.float32)),
        grid=(BATCH, n_lat),
        in_specs=[lat(D_NA), lat(D_RW), allt(D_RW), allt(D_RW), lat(D_MODEL),
                  pl.BlockSpec((1, 6, D_MODEL), lambda b, j: (b, 0, 0)),
                  full((1, D_RW)), full((1, D_RW)), full((D_MODEL, D_MODEL)), full((1, D_MODEL)),
                  full((D_RW, D_RW)), full((LOGIT_ROWS, D_MODEL)), full((LOGIT_ROWS, 1))],
        out_specs=(lat(D_MODEL), lat(D_MODEL),
                   pl.BlockSpec((2, TOK_TILE), lambda b, j: (0, b * n_lat + j)),
                   pl.BlockSpec((2, TOK_TILE), lambda b, j: (0, b * n_lat + j))),
        compiler_params=pltpu.CompilerParams(vmem_limit_bytes=VMEM_LIMIT),
        name="finish",
    )(na_x, o_rw, bonus, g, x, mod3, ln_g, ln_b, w_out_bf16, norm2_g, bd, wr_t, br)


def _moe_kernel(te_ref, tv_ref, xs_ref, w1_ref, w3_ref, w2_ref, ws_ref, y_ref):
    t = pl.program_id(0)

    @pl.when(tv_ref[t] == 1)
    def _():
        xs = xs_ref[...]
        h1 = jnp.dot(xs, w1_ref[0], preferred_element_type=jnp.float32)
        h3 = jnp.dot(xs, w3_ref[0], preferred_element_type=jnp.float32)
        he = (_silu(h1) * h3).astype(jnp.bfloat16)
        y_ref[...] = jnp.dot(he, w2_ref[0], preferred_element_type=jnp.float32) * ws_ref[...]

    @pl.when(tv_ref[t] == 0)
    def _():
        y_ref[...] = jnp.zeros_like(y_ref)


def _moe_call(tile_e, tile_valid, xs, w1, w3, w2, w_slot):
    return pl.pallas_call(
        _moe_kernel,
        out_shape=jax.ShapeDtypeStruct((N_SLOTS, D_MODEL), jnp.float32),
        grid_spec=pltpu.PrefetchScalarGridSpec(
            num_scalar_prefetch=2,
            grid=(N_MOE_TILES,),
            in_specs=[pl.BlockSpec((MOE_TILE, D_MODEL), lambda t, te, tv: (t, 0)),
                      pl.BlockSpec((1, D_MODEL, D_EXPERT), lambda t, te, tv: (te[t], 0, 0)),
                      pl.BlockSpec((1, D_MODEL, D_EXPERT), lambda t, te, tv: (te[t], 0, 0)),
                      pl.BlockSpec((1, D_EXPERT, D_MODEL), lambda t, te, tv: (te[t], 0, 0)),
                      pl.BlockSpec((MOE_TILE, 1), lambda t, te, tv: (t, 0))],
            out_specs=pl.BlockSpec((MOE_TILE, D_MODEL), lambda t, te, tv: (t, 0))),
        compiler_params=pltpu.CompilerParams(vmem_limit_bytes=VMEM_LIMIT),
        name="moe",
    )(tile_e, tile_valid, xs, w1, w3, w2, w_slot)


def _moe_plan(eid, wts):
    e_flat = eid.reshape(-1)
    onehot = (e_flat[:, None] == jnp.arange(N_EXPERTS)[None, :]).astype(jnp.int32)
    csum = jnp.cumsum(onehot, axis=0)
    counts = csum[-1]
    rank = jnp.take_along_axis(csum, e_flat[:, None], axis=1)[:, 0] - 1
    gsz = ((counts + MOE_TILE - 1) // MOE_TILE) * MOE_TILE
    gend = jnp.cumsum(gsz)
    goff = gend - gsz
    pos = goff[e_flat] + rank
    tok = jnp.tile(jnp.arange(N_TOK, dtype=jnp.int32), 2)
    src_tok = jnp.zeros((N_SLOTS,), jnp.int32).at[pos].set(tok)
    w_slot = jnp.zeros((N_SLOTS,), jnp.float32).at[pos].set(wts.reshape(-1))
    starts = jnp.arange(N_MOE_TILES, dtype=jnp.int32) * MOE_TILE
    te = jnp.searchsorted(gend, starts, side="right").astype(jnp.int32)
    valid = (te < N_EXPERTS).astype(jnp.int32)
    last = jnp.max(jnp.where(counts > 0, jnp.arange(N_EXPERTS), 0)).astype(jnp.int32)
    te = jnp.where(valid == 1, te, last)
    return pos, src_tok, w_slot.reshape(N_SLOTS, 1), te, valid


def _final_kernel(x1_ref, ya_ref, yb_ref, mod_ref, o_ref):
    o_ref[0] = x1_ref[0] + mod_ref[0][5:6, :] * (ya_ref[0] + yb_ref[0])


def _final_call(x1, ya, yb, mod3):
    n_lat = SEQ // TOK_TILE
    lat = pl.BlockSpec((1, TOK_TILE, D_MODEL), lambda b, j: (b, j, 0))
    return pl.pallas_call(
        _final_kernel,
        out_shape=jax.ShapeDtypeStruct((BATCH, SEQ, D_MODEL), jnp.float32),
        grid=(BATCH, n_lat),
        in_specs=[lat, lat, lat, pl.BlockSpec((1, 6, D_MODEL), lambda b, j: (b, 0, 0))],
        out_specs=lat,
        name="final",
    )(x1, ya, yb, mod3)


def _block_diag2(w):
    z = jnp.zeros_like(w[0])
    return jnp.concatenate([jnp.concatenate([w[0], z], axis=1), jnp.concatenate([z, w[1]], axis=1)], axis=0)


def kernel(x, c, ctx, c_ctx, w_mod, b_mod, norm1_g, norm2_g, w_in, na_q_g, na_k_g, na_rpb, rw_mu_prev, rw_mu_next,
           rw_w0, rw_w_up, rw_a0, rw_a_up, rw_g_up, rw_k_k, rw_k_a, rw_r_k, rw_ln_g, rw_ln_b, w_out, moe_wg, moe_bg,
           moe_we, moe_be, moe_w1, moe_w3, moe_w2):
    bf = jnp.bfloat16
    mod_rows = BATCH + 8
    cs = jnp.concatenate([c, c_ctx[None, :], jnp.zeros((mod_rows - BATCH - 1, D_MODEL), jnp.float32)], axis=0)
    mod = _mod_call(cs, w_mod[0], b_mod[0][None, :])
    mod3 = mod.reshape(mod_rows, 6, D_MODEL)

    p_na, p_rw = _inproj_call(x, ctx, mod3, norm1_g[0][None, :], w_in[0].astype(bf))

    na_x = _na_call(p_na, _na_bias_table(na_rpb[0]), na_q_g[0][None, :], na_k_g[0][None, :])

    cos_t, sin_t = _rope_tables()
    head = jnp.arange(D_RW) // HEAD_DIM
    bd = (head[:, None] == head[None, :]).astype(bf)
    r, kk, v, dec, akk, kw, g, bonus = _prep_call(
        p_rw, cos_t, sin_t, rw_mu_prev[0][None, :], rw_mu_next[0][None, :],
        rw_w0[0].reshape(1, 2 * D_RW), _block_diag2(rw_w_up[0]).astype(bf),
        rw_a0[0].reshape(1, 2 * D_RW), _block_diag2(rw_a_up[0]).astype(bf),
        rw_g_up[0].astype(bf), rw_k_k[0][None, :], rw_k_a[0][None, :], rw_r_k[0].reshape(1, D_RW), bd)

    o2 = _scan_call(_to_scan_layout(r)[0], _to_scan_layout(kk)[0], _to_scan_layout(v)[0],
                    _to_scan_layout(dec), _to_scan_layout(akk), _to_scan_layout(kw))
    o_rw = _from_scan_layout(o2[0] + o2[1])

    wr_t = jnp.zeros((LOGIT_ROWS, D_MODEL), jnp.float32)
    wr_t = wr_t.at[0:MOE_GROUPS].set(moe_wg[0].T)
    wr_t = wr_t.at[8:].set(moe_we[0].transpose(0, 2, 1).reshape(N_EXPERTS, D_MODEL))
    br = jnp.zeros((LOGIT_ROWS,), jnp.float32).at[0:MOE_GROUPS].set(moe_bg[0]).at[8:].set(moe_be[0].reshape(-1))
    x1, h2, eid, wts = _finish_call(na_x, o_rw, bonus, g, x, mod3, rw_ln_g[0][None, :], rw_ln_b[0][None, :],
                                    w_out[0].astype(bf), norm2_g[0][None, :], bd, wr_t, br[:, None])

    pos, src_tok, w_slot, tile_e, tile_valid = _moe_plan(eid, wts)
    xs = jnp.take(h2.reshape(N_TOK, D_MODEL), src_tok, axis=0)
    ys = _moe_call(tile_e, tile_valid, xs, moe_w1[0].astype(bf), moe_w3[0].astype(bf), moe_w2[0].astype(bf), w_slot)
    ya = jnp.take(ys, pos[:N_TOK], axis=0).reshape(BATCH, SEQ, D_MODEL)
    yb = jnp.take(ys, pos[N_TOK:], axis=0).reshape(BATCH, SEQ, D_MODEL)
    return _final_call(x1, ya, yb, mod3)
```

```python
import functools
import math

import jax
import jax.numpy as jnp
import numpy as np
from jax import lax
from jax.experimental import pallas as pl
from jax.experimental.pallas import tpu as pltpu

D_MODEL = 1024
BATCH = 16
SEQ = 2048
GRID_W = 64
GRID_H = SEQ // GRID_W
CTX_LEN = 256
T_ALL = CTX_LEN + SEQ
HEAD_DIM = 64
D_NA = 512
D_RW = 512
H_NA = D_NA // HEAD_DIM
H_RW = D_RW // HEAD_DIM
NA_KH = 8
NA_KW = 16
LORA_W = 64
LORA_A = 64
LORA_G = 128
D_RW_IN = 3 * D_RW + 2 * (LORA_W + LORA_A) + LORA_G
D_IN = 3 * D_NA + D_RW_IN
MOE_GROUPS = 4
MOE_PER_GROUP = 8
N_EXPERTS = MOE_GROUPS * MOE_PER_GROUP
D_EXPERT = 512
ROPE_THETA = 10000.0
NORM_EPS = 1e-6
RW_LN_EPS = 64e-5
NEG = -1e30
DECAY_SCALE = math.exp(-0.5)

LANES = 128
TOK_TILE = 256
N_TILES = T_ALL // TOK_TILE
SCAN_CHUNK = 16
MOE_TILE = 256
N_TOK = BATCH * SEQ
N_SLOTS = 2 * N_TOK + N_EXPERTS * MOE_TILE
N_MOE_TILES = N_SLOTS // MOE_TILE
LOGIT_ROWS = 8 + N_EXPERTS
VMEM_LIMIT = 48 * 1024 * 1024

_HI = lax.Precision.HIGHEST


def _sigmoid(x):
    return 1.0 / (1.0 + jnp.exp(-x))


def _silu(x):
    return x * _sigmoid(x)


def _rmsnorm_rows(xf, g):
    return xf * lax.rsqrt(jnp.mean(xf * xf, axis=-1, keepdims=True) + NORM_EPS) * g


def _group_sum(x, bd):
    hi = x.astype(jnp.bfloat16)
    lo = (x - hi.astype(jnp.float32)).astype(jnp.bfloat16)
    return (jnp.dot(hi, bd, preferred_element_type=jnp.float32)
            + jnp.dot(lo, bd, preferred_element_type=jnp.float32))


def _mod_kernel(c_ref, w_ref, b_ref, o_ref):
    o_ref[...] = jnp.dot(_silu(c_ref[...]), w_ref[...], precision=_HI,
                         preferred_element_type=jnp.float32) + b_ref[...]


def _mod_call(cs, w_mod, b_mod):
    rows = cs.shape[0]
    n = w_mod.shape[1]
    blk = 1024
    return pl.pallas_call(
        _mod_kernel,
        out_shape=jax.ShapeDtypeStruct((rows, n), jnp.float32),
        grid=(n // blk,),
        in_specs=[pl.BlockSpec((rows, D_MODEL), lambda j: (0, 0)),
                  pl.BlockSpec((D_MODEL, blk), lambda j: (0, j)),
                  pl.BlockSpec((1, blk), lambda j: (0, j))],
        out_specs=pl.BlockSpec((rows, blk), lambda j: (0, j)),
        name="mod",
    )(cs, w_mod, b_mod)


def _inproj_kernel(x_ref, ctx_ref, modx_ref, modc_ref, g_ref, w_ref, qg_ref, kg_ref, bd_ref, na_ref, rw_ref):
    is_ctx = pl.program_id(1) == 0
    xin = jnp.where(is_ctx, ctx_ref[0], x_ref[0])
    mod = jnp.where(is_ctx, modc_ref[0], modx_ref[0])
    h = _rmsnorm_rows(xin, g_ref[...]) * (1.0 + mod[1:2, :]) + mod[0:1, :]
    p = jnp.dot(h.astype(jnp.bfloat16), w_ref[...], preferred_element_type=jnp.float32)
    rw_ref[0] = p[:, 3 * D_NA:]
    bd = bd_ref[...]
    q = p[:, 0:D_NA]
    k = p[:, D_NA:2 * D_NA]
    qn = q * lax.rsqrt(_group_sum(q * q, bd) * (1.0 / HEAD_DIM) + NORM_EPS) * (qg_ref[...] * HEAD_DIM ** -0.5)
    kn = k * lax.rsqrt(_group_sum(k * k, bd) * (1.0 / HEAD_DIM) + NORM_EPS) * kg_ref[...]
    na_ref[0, :, 0:D_NA] = qn.astype(jnp.bfloat16)
    na_ref[0, :, D_NA:2 * D_NA] = kn.astype(jnp.bfloat16)
    na_ref[0, :, 2 * D_NA:] = p[:, 2 * D_NA:3 * D_NA].astype(jnp.bfloat16)


def _inproj_call(x, ctx, mod3, norm_g, w_in_bf16, q_g, k_g, bd):
    return pl.pallas_call(
        _inproj_kernel,
        out_shape=(jax.ShapeDtypeStruct((BATCH, T_ALL, 3 * D_NA), jnp.bfloat16),
                   jax.ShapeDtypeStruct((BATCH, T_ALL, D_RW_IN), jnp.float32)),
        grid=(BATCH, N_TILES),
        in_specs=[pl.BlockSpec((1, TOK_TILE, D_MODEL), lambda b, j: (b, jnp.maximum(j - 1, 0), 0)),
                  pl.BlockSpec((1, TOK_TILE, D_MODEL), lambda b, j: (b, 0, 0)),
                  pl.BlockSpec((1, 6, D_MODEL), lambda b, j: (b, 0, 0)),
                  pl.BlockSpec((1, 6, D_MODEL), lambda b, j: (BATCH, 0, 0)),
                  pl.BlockSpec((1, D_MODEL), lambda b, j: (0, 0)),
                  pl.BlockSpec((D_MODEL, D_IN), lambda b, j: (0, 0)),
                  pl.BlockSpec((1, D_NA), lambda b, j: (0, 0)),
                  pl.BlockSpec((1, D_NA), lambda b, j: (0, 0)),
                  pl.BlockSpec((D_NA, D_NA), lambda b, j: (0, 0))],
        out_specs=(pl.BlockSpec((1, TOK_TILE, 3 * D_NA), lambda b, j: (b, j, 0)),
                   pl.BlockSpec((1, TOK_TILE, D_RW_IN), lambda b, j: (b, j, 0))),
        compiler_params=pltpu.CompilerParams(vmem_limit_bytes=VMEM_LIMIT),
        name="inproj",
    )(x, ctx, mod3, mod3, norm_g, w_in_bf16, q_g, k_g, bd)


NA_QROWS = 4
NA_QBLK = NA_QROWS * GRID_W
NA_BAND = NA_KH + NA_QROWS - 1
NA_BAND_KEYS = NA_BAND * GRID_W
NA_NBLK = GRID_H // NA_QROWS


def _na_band_start(i):
    return np.clip(i * NA_QROWS - NA_KH // 2, 0, GRID_H - NA_BAND)


def _na_kernel(q_ref, k_ref, v_ref, bias_ref, o_ref, qs, ks, vs):
    for h in range(2):
        hs = slice(h * HEAD_DIM, (h + 1) * HEAD_DIM)
        qs[h] = q_ref[0, CTX_LEN:, hs]
        ks[h] = k_ref[0, :, hs]
        vs[h] = v_ref[0, :, hs]

    nt = (((1,), (1,)), ((), ()))

    def body(i, carry):
        bs = jnp.clip(i * NA_QROWS - NA_KH // 2, 0, GRID_H - NA_BAND)
        pattern = jnp.where(i == 0, 0, jnp.where(i == NA_NBLK - 1, 2, 1))
        q0 = pl.multiple_of(i * NA_QBLK, NA_QBLK)
        k0 = pl.multiple_of(CTX_LEN + bs * GRID_W, GRID_W)
        for h in range(2):
            qb = qs[h, pl.ds(q0, NA_QBLK), :]
            kw = ks[h, pl.ds(k0, NA_BAND_KEYS), :]
            vw = vs[h, pl.ds(k0, NA_BAND_KEYS), :]
            kc = ks[h, 0:CTX_LEN, :]
            vc = vs[h, 0:CTX_LEN, :]
            s_win = lax.dot_general(qb, kw, nt, preferred_element_type=jnp.float32) + bias_ref[h, pattern]
            s_ctx = lax.dot_general(qb, kc, nt, preferred_element_type=jnp.float32)
            m = jnp.maximum(jnp.max(s_win, axis=-1, keepdims=True), jnp.max(s_ctx, axis=-1, keepdims=True))
            e_win = jnp.exp(s_win - m)
            e_ctx = jnp.exp(s_ctx - m)
            den = jnp.sum(e_win, axis=-1, keepdims=True) + jnp.sum(e_ctx, axis=-1, keepdims=True)
            o = (jnp.dot(e_win.astype(jnp.bfloat16), vw, preferred_element_type=jnp.float32)
                 + jnp.dot(e_ctx.astype(jnp.bfloat16), vc, preferred_element_type=jnp.float32))
            o_ref[0, pl.ds(q0, NA_QBLK), h * HEAD_DIM:(h + 1) * HEAD_DIM] = o / den
        return carry

    lax.fori_loop(0, NA_NBLK, body, 0)


def _na_call(qkv, bias8):
    n_hp = D_NA // LANES
    return pl.pallas_call(
        _na_kernel,
        out_shape=jax.ShapeDtypeStruct((BATCH, SEQ, D_NA), jnp.float32),
        grid=(n_hp, BATCH),
        in_specs=[pl.BlockSpec((1, T_ALL, LANES), lambda hp, b: (b, 0, hp)),
                  pl.BlockSpec((1, T_ALL, LANES), lambda hp, b: (b, 0, n_hp + hp)),
                  pl.BlockSpec((1, T_ALL, LANES), lambda hp, b: (b, 0, 2 * n_hp + hp)),
                  pl.BlockSpec((2, 3, NA_QBLK, NA_BAND_KEYS), lambda hp, b: (hp, 0, 0, 0))],
        out_specs=pl.BlockSpec((1, SEQ, LANES), lambda hp, b: (b, 0, hp)),
        scratch_shapes=[pltpu.VMEM((2, SEQ, HEAD_DIM), jnp.bfloat16),
                        pltpu.VMEM((2, T_ALL, HEAD_DIM), jnp.bfloat16),
                        pltpu.VMEM((2, T_ALL, HEAD_DIM), jnp.bfloat16)],
        compiler_params=pltpu.CompilerParams(vmem_limit_bytes=VMEM_LIMIT),
        name="na",
    )(qkv, qkv, qkv, bias8)


def _na_bias_table(rpb):
    qc = np.arange(GRID_W)
    cs = np.clip(qc - NA_KW // 2, 0, GRID_W - NA_KW)
    kc = np.arange(GRID_W)
    col_ok = (kc[None, :] >= cs[:, None]) & (kc[None, :] < cs[:, None] + NA_KW)
    dc = np.clip(kc[None, :] - qc[:, None], -(NA_KW - 1), NA_KW - 1) + NA_KW - 1
    tz = jnp.where(col_ok[None, None], rpb[:, :, dc], NEG)
    masked = jnp.full((H_NA, GRID_W, GRID_W), NEG, jnp.float32)
    patterns = []
    for i in (0, 1, NA_NBLK - 1):
        bs = int(_na_band_start(i))
        rows = []
        for g in range(NA_QROWS):
            qr = i * NA_QROWS + g
            rs = int(np.clip(qr - NA_KH // 2, 0, GRID_H - NA_KH))
            blocks = []
            for j in range(NA_BAND):
                kr = bs + j
                blocks.append(tz[:, kr - qr + NA_KH - 1] if rs <= kr < rs + NA_KH else masked)
            rows.append(jnp.concatenate(blocks, axis=-1))
        patterns.append(jnp.concatenate(rows, axis=-2))
    return jnp.stack(patterns, axis=1).astype(jnp.float32)


def _swap16(x):
    lane = lax.broadcasted_iota(jnp.int32, x.shape, 1)
    return jnp.where((lane & 16) == 0, pltpu.roll(x, LANES - 16, axis=1), pltpu.roll(x, 16, axis=1))


def _rope(x, cos, sin):
    blocks = []
    for i in range(x.shape[1] // LANES):
        xb = x[:, i * LANES:(i + 1) * LANES]
        blocks.append(xb * cos + _swap16(xb) * sin)
    return jnp.concatenate(blocks, axis=1)


def _prep_kernel(p_ref, pv_ref, nx_ref, cos_ref, sin_ref, mup_ref, mun_ref, w0_ref, wup_ref, a0_ref, aup_ref,
                 gup_ref, kk_ref, ka_ref, rk_ref, bd_ref,
                 r_out, kkn_out, v_out, dec_out, akk_out, kw_out, g_out, bonus_out):
    j = pl.program_id(1)
    p = p_ref[0]
    row = lax.broadcasted_iota(jnp.int32, p.shape, 0)
    prev_row = jnp.where((j == 0) | (j == 1), 0.0, pv_ref[0, 7:8, :])
    next_row = jnp.where((j == 0) | (j == N_TILES - 1), 0.0, nx_ref[0, 0:1, :])
    prev = jnp.where(row == 0, prev_row, pltpu.roll(p, 1, axis=0))
    nxt = jnp.where(row == TOK_TILE - 1, next_row, pltpu.roll(p, TOK_TILE - 1, axis=0))
    ps = p + mup_ref[...] * (prev - p) + mun_ref[...] * (nxt - p)

    cos = cos_ref[...]
    sin = sin_ref[...]
    r = _rope(ps[:, 0:D_RW], cos, sin)
    k = _rope(ps[:, D_RW:2 * D_RW], cos, sin)
    v = ps[:, 2 * D_RW:3 * D_RW]
    o3 = 3 * D_RW
    wd = ps[:, o3:o3 + 2 * LORA_W]
    ad = ps[:, o3 + 2 * LORA_W:o3 + 2 * LORA_W + 2 * LORA_A]
    gd = ps[:, o3 + 2 * LORA_W + 2 * LORA_A:]

    w_pre = w0_ref[...] + jnp.dot(jnp.tanh(wd).astype(jnp.bfloat16), wup_ref[...], preferred_element_type=jnp.float32)
    decay = jnp.exp(-DECAY_SCALE * _sigmoid(w_pre))
    a = _sigmoid(a0_ref[...] + jnp.dot(ad.astype(jnp.bfloat16), aup_ref[...], preferred_element_type=jnp.float32))
    g = jnp.dot(_sigmoid(gd).astype(jnp.bfloat16), gup_ref[...], preferred_element_type=jnp.float32)

    bd = bd_ref[...]
    kk = k * kk_ref[...]
    kk = kk / jnp.maximum(jnp.sqrt(_group_sum(kk * kk, bd)), 1e-12)
    bonus = _group_sum(r * k * rk_ref[...], bd) * v
    ka = ka_ref[...]

    r_out[0] = r
    kkn_out[0] = kk
    v_out[0] = v
    g_out[0] = g
    bonus_out[0] = bonus
    for d in range(2):
        a_d = a[:, d * D_RW:(d + 1) * D_RW]
        dec_out[0, :, d * D_RW:(d + 1) * D_RW] = decay[:, d * D_RW:(d + 1) * D_RW]
        akk_out[0, :, d * D_RW:(d + 1) * D_RW] = kk * a_d
        kw_out[0, :, d * D_RW:(d + 1) * D_RW] = k * (1.0 + (a_d - 1.0) * ka)


def _prep_call(p_rw, cos_t, sin_t, mu_prev, mu_next, w0, wup2, a0, aup2, gup, k_k, k_a, r_k, bd):
    sub = TOK_TILE // 8
    n_sub = T_ALL // 8
    full = lambda shape: pl.BlockSpec(shape, lambda b, j: (0,) * len(shape))
    tok = lambda width: pl.BlockSpec((1, TOK_TILE, width), lambda b, j: (b, j, 0))
    one = jax.ShapeDtypeStruct((BATCH, T_ALL, D_RW), jnp.float32)
    two = jax.ShapeDtypeStruct((BATCH, T_ALL, 2 * D_RW), jnp.float32)
    return pl.pallas_call(
        _prep_kernel,
        out_shape=(one, one, one, two, two, two, one, one),
        grid=(BATCH, N_TILES),
        in_specs=[tok(D_RW_IN),
                  pl.BlockSpec((1, 8, D_RW_IN), lambda b, j: (b, jnp.maximum(j * sub - 1, 0), 0)),
                  pl.BlockSpec((1, 8, D_RW_IN), lambda b, j: (b, jnp.minimum((j + 1) * sub, n_sub - 1), 0)),
                  pl.BlockSpec((TOK_TILE, LANES), lambda b, j: (j, 0)),
                  pl.BlockSpec((TOK_TILE, LANES), lambda b, j: (j, 0)),
                  full((1, D_RW_IN)), full((1, D_RW_IN)),
                  full((1, 2 * D_RW)), full((2 * LORA_W, 2 * D_RW)),
                  full((1, 2 * D_RW)), full((2 * LORA_A, 2 * D_RW)),
                  full((LORA_G, D_RW)),
                  full((1, D_RW)), full((1, D_RW)), full((1, D_RW)),
                  full((D_RW, D_RW))],
        out_specs=(tok(D_RW), tok(D_RW), tok(D_RW), tok(2 * D_RW), tok(2 * D_RW), tok(2 * D_RW), tok(D_RW), tok(D_RW)),
        compiler_params=pltpu.CompilerParams(vmem_limit_bytes=VMEM_LIMIT),
        name="prep",
    )(p_rw, p_rw, p_rw, cos_t, sin_t, mu_prev, mu_next, w0, wup2, a0, aup2, gup, k_k, k_a, r_k, bd)


def _rope_tables():
    nf = HEAD_DIM // 4
    pos = np.arange(SEQ)
    inv = ROPE_THETA ** (-np.arange(nf, dtype=np.float32) / nf)
    lane = np.arange(LANES) % HEAD_DIM
    half, pair, f = lane // 32, (lane % 32) // 16, lane % 16
    coord = np.where(half[None, :] == 0, (pos // GRID_W)[:, None], (pos % GRID_W)[:, None]).astype(np.float32)
    ang = coord * inv[f][None, :].astype(np.float32)
    cos = np.cos(ang).astype(np.float32)
    sin = np.sin(ang).astype(np.float32) * np.where(pair == 0, -1.0, 1.0)[None, :].astype(np.float32)
    cos = np.concatenate([np.ones((CTX_LEN, LANES), np.float32), cos], axis=0)
    sin = np.concatenate([np.zeros((CTX_LEN, LANES), np.float32), sin], axis=0)
    return jnp.asarray(cos), jnp.asarray(sin)


def _scan_kernel(r_ref, kk_ref, v_ref, w_ref, akk_ref, kw_ref, o_ref, s_ref):
    d = pl.program_id(0)
    i = pl.program_id(1)

    @pl.when(i == 0)
    def _():
        s_ref[...] = jnp.zeros_like(s_ref)

    k_unroll = 8

    def step(s, carry):
        t = jnp.where(d == 0, s, SCAN_CHUNK - 1 - s)
        vv = v_ref[t]

        def sk_body(kb, sk):
            for u in range(k_unroll):
                k = kb * k_unroll + u
                sk = sk + s_ref[k] * kk_ref[t, pl.ds(k, 1), :]
            return sk

        sk = lax.fori_loop(0, HEAD_DIM // k_unroll, sk_body, jnp.zeros((HEAD_DIM, LANES), jnp.float32))

        def upd_body(kb, o):
            for u in range(k_unroll):
                k = kb * k_unroll + u
                s_new = (s_ref[k] * w_ref[0, t, pl.ds(k, 1), :] - sk * akk_ref[0, t, pl.ds(k, 1), :]
                         + vv * kw_ref[0, t, pl.ds(k, 1), :])
                s_ref[k] = s_new
                o = o + s_new * r_ref[t, pl.ds(k, 1), :]
            return o

        o_ref[0, t] = lax.fori_loop(0, HEAD_DIM // k_unroll, upd_body, jnp.zeros((HEAD_DIM, LANES), jnp.float32))
        return carry

    lax.fori_loop(0, SCAN_CHUNK, step, 0)


def _scan_call(r_t, kk_t, v_t, w_t, akk_t, kw_t):
    nc = CTX_LEN // SCAN_CHUNK
    nl = SEQ // SCAN_CHUNK

    def blk(d, i):
        rev = jnp.where(i < nc, nc - 1 - i, 2 * nc + nl - 1 - i)
        return jnp.where(d == 0, i, rev)

    def oblk(d, i):
        fwd = jnp.maximum(i - nc, 0)
        rev = jnp.where(i < nc, nl - 1, nc + nl - 1 - i)
        return jnp.where(d == 0, fwd, rev)

    shared = pl.BlockSpec((SCAN_CHUNK, HEAD_DIM, LANES), lambda d, i: (blk(d, i), 0, 0))
    perdir = pl.BlockSpec((1, SCAN_CHUNK, HEAD_DIM, LANES), lambda d, i: (d, blk(d, i), 0, 0))
    return pl.pallas_call(
        _scan_kernel,
        out_shape=jax.ShapeDtypeStruct((2, SEQ, HEAD_DIM, LANES), jnp.float32),
        grid=(2, nc + nl),
        in_specs=[shared, shared, shared, perdir, perdir, perdir],
        out_specs=pl.BlockSpec((1, SCAN_CHUNK, HEAD_DIM, LANES), lambda d, i: (d, oblk(d, i), 0, 0)),
        scratch_shapes=[pltpu.VMEM((HEAD_DIM, HEAD_DIM, LANES), jnp.float32)],
        compiler_params=pltpu.CompilerParams(dimension_semantics=("arbitrary", "arbitrary"),
                                             vmem_limit_bytes=VMEM_LIMIT),
        name="scan",
    )(r_t, kk_t, v_t, w_t, akk_t, kw_t)


def _to_scan_layout(a):
    n = a.shape[-1] // D_RW
    a = a.reshape(BATCH, T_ALL, n, H_RW, HEAD_DIM)
    return a.transpose(2, 1, 4, 0, 3).reshape(n, T_ALL, HEAD_DIM, LANES)


def _from_scan_layout(o):
    return o.reshape(SEQ, HEAD_DIM, BATCH, H_RW).transpose(2, 0, 3, 1).reshape(BATCH, SEQ, D_RW)


def _finish_kernel(na_ref, o_ref, bonus_ref, g_ref, x_ref, mod_ref, lng_ref, lnb_ref, wout_ref, n2g_ref, bd_ref,
                   wr_ref, br_ref, x1_out, h2_out, eid_out, wt_out):
    bd = bd_ref[...]
    o = o_ref[0]
    mu = _group_sum(o, bd) * (1.0 / HEAD_DIM)
    oc = o - mu
    var = _group_sum(oc * oc, bd) * (1.0 / HEAD_DIM)
    y = oc * lax.rsqrt(var + RW_LN_EPS) * lng_ref[...] + lnb_ref[...]
    rw = (y + bonus_ref[0]) * g_ref[0]
    mix = jnp.concatenate([na_ref[0], rw], axis=-1).astype(jnp.bfloat16)
    yx = jnp.dot(mix, wout_ref[...], preferred_element_type=jnp.float32)
    mod = mod_ref[0]
    x1 = x_ref[0] + mod[2:3, :] * yx
    x1_out[0] = x1
    h2 = _rmsnorm_rows(x1, n2g_ref[...]) * (1.0 + mod[4:5, :]) + mod[3:4, :]
    h2_out[0] = h2.astype(jnp.bfloat16)

    lg = lax.dot_general(wr_ref[...], h2, (((1,), (1,)), ((), ())), precision=_HI,
                         preferred_element_type=jnp.float32) + br_ref[...]
    gl = lg[0:MOE_GROUPS, :]
    grow = lax.broadcasted_iota(jnp.int32, gl.shape, 0)
    gmax = jnp.max(gl, axis=0, keepdims=True)
    g_sel = jnp.min(jnp.where(gl == gmax, grow, MOE_GROUPS), axis=0, keepdims=True)
    p_group = 1.0 / jnp.sum(jnp.exp(gl - gmax), axis=0, keepdims=True)
    el = jnp.zeros((MOE_PER_GROUP, TOK_TILE), jnp.float32)
    for gi in range(MOE_GROUPS):
        el = jnp.where(g_sel == gi, lg[8 + gi * MOE_PER_GROUP:8 + (gi + 1) * MOE_PER_GROUP, :], el)
    erow = lax.broadcasted_iota(jnp.int32, el.shape, 0)
    m1 = jnp.max(el, axis=0, keepdims=True)
    i1 = jnp.min(jnp.where(el == m1, erow, MOE_PER_GROUP), axis=0, keepdims=True)
    el2 = jnp.where(erow == i1, -jnp.inf, el)
    m2 = jnp.max(el2, axis=0, keepdims=True)
    i2 = jnp.min(jnp.where(el2 == m2, erow, MOE_PER_GROUP), axis=0, keepdims=True)
    e21 = jnp.exp(m2 - m1)
    w1 = p_group / (1.0 + e21)
    w2 = p_group * e21 / (1.0 + e21)
    eid_out[0:1, :] = g_sel * MOE_PER_GROUP + i1
    eid_out[1:2, :] = g_sel * MOE_PER_GROUP + i2
    wt_out[0:1, :] = w1
    wt_out[1:2, :] = w2


def _finish_call(na_x, o_rw, bonus, g, x, mod3, ln_g, ln_b, w_out_bf16, norm2_g, bd, wr_t, br):
    n_lat = SEQ // TOK_TILE
    full = lambda shape: pl.BlockSpec(shape, lambda b, j: (0,) * len(shape))
    lat = lambda width: pl.BlockSpec((1, TOK_TILE, width), lambda b, j: (b, j, 0))
    allt = lambda width: pl.BlockSpec((1, TOK_TILE, width), lambda b, j: (b, j + 1, 0))
    return pl.pallas_call(
        _finish_kernel,
        out_shape=(jax.ShapeDtypeStruct((BATCH, SEQ, D_MODEL), jnp.float32),
                   jax.ShapeDtypeStruct((BATCH, SEQ, D_MODEL), jnp.bfloat16),
                   jax.ShapeDtypeStruct((2, N_TOK), jnp.int32),
                   jax.ShapeDtypeStruct((2, N_TOK), jnp.float32)),
        grid=(BATCH, n_lat),
        in_specs=[lat(D_NA), lat(D_RW), allt(D_RW), allt(D_RW), lat(D_MODEL),
                  pl.BlockSpec((1, 6, D_MODEL), lambda b, j: (b, 0, 0)),
                  full((1, D_RW)), full((1, D_RW)), full((D_MODEL, D_MODEL)), full((1, D_MODEL)),
                  full((D_RW, D_RW)), full((LOGIT_ROWS, D_MODEL)), full((LOGIT_ROWS, 1))],
        out_specs=(lat(D_MODEL), lat(D_MODEL),
                   pl.BlockSpec((2, TOK_TILE), lambda b, j: (0, b * n_lat + j)),
                   pl.BlockSpec((2, TOK_TILE), lambda b, j: (0, b * n_lat + j))),
        compiler_params=pltpu.CompilerParams(vmem_limit_bytes=VMEM_LIMIT),
        name="finish",
    )(na_x, o_rw, bonus, g, x, mod3, ln_g, ln_b, w_out_bf16, norm2_g, bd, wr_t, br)


def _moe_kernel(te_ref, tv_ref, xs_ref, w1_ref, w3_ref, w2_ref, ws_ref, y_ref, w1s, w3s, w2s):
    t = pl.program_id(0)

    @pl.when((t == 0) | (te_ref[t] != te_ref[jnp.maximum(t - 1, 0)]))
    def _():
        w1s[...] = w1_ref[0].astype(jnp.bfloat16)
        w3s[...] = w3_ref[0].astype(jnp.bfloat16)
        w2s[...] = w2_ref[0].astype(jnp.bfloat16)

    @pl.when(tv_ref[t] == 1)
    def _():
        xs = xs_ref[...]
        h1 = jnp.dot(xs, w1s[...], preferred_element_type=jnp.float32)
        h3 = jnp.dot(xs, w3s[...], preferred_element_type=jnp.float32)
        he = (_silu(h1) * h3).astype(jnp.bfloat16)
        y_ref[...] = jnp.dot(he, w2s[...], preferred_element_type=jnp.float32) * ws_ref[...]

    @pl.when(tv_ref[t] == 0)
    def _():
        y_ref[...] = jnp.zeros_like(y_ref)


def _moe_call(tile_e, tile_valid, xs, w1, w3, w2, w_slot):
    return pl.pallas_call(
        _moe_kernel,
        out_shape=jax.ShapeDtypeStruct((N_SLOTS, D_MODEL), jnp.float32),
        grid_spec=pltpu.PrefetchScalarGridSpec(
            num_scalar_prefetch=2,
            grid=(N_MOE_TILES,),
            in_specs=[pl.BlockSpec((MOE_TILE, D_MODEL), lambda t, te, tv: (t, 0)),
                      pl.BlockSpec((1, D_MODEL, D_EXPERT), lambda t, te, tv: (te[t], 0, 0)),
                      pl.BlockSpec((1, D_MODEL, D_EXPERT), lambda t, te, tv: (te[t], 0, 0)),
                      pl.BlockSpec((1, D_EXPERT, D_MODEL), lambda t, te, tv: (te[t], 0, 0)),
                      pl.BlockSpec((MOE_TILE, 1), lambda t, te, tv: (t, 0))],
            out_specs=pl.BlockSpec((MOE_TILE, D_MODEL), lambda t, te, tv: (t, 0)),
            scratch_shapes=[pltpu.VMEM((D_MODEL, D_EXPERT), jnp.bfloat16),
                            pltpu.VMEM((D_MODEL, D_EXPERT), jnp.bfloat16),
                            pltpu.VMEM((D_EXPERT, D_MODEL), jnp.bfloat16)]),
        compiler_params=pltpu.CompilerParams(dimension_semantics=("arbitrary",), vmem_limit_bytes=VMEM_LIMIT),
        name="moe",
    )(tile_e, tile_valid, xs, w1, w3, w2, w_slot)


def _moe_plan(eid, wts):
    e_flat = eid.reshape(-1)
    onehot = (e_flat[:, None] == jnp.arange(N_EXPERTS)[None, :]).astype(jnp.int32)
    csum = jnp.cumsum(onehot, axis=0)
    counts = csum[-1]
    rank = jnp.take_along_axis(csum, e_flat[:, None], axis=1)[:, 0] - 1
    gsz = ((counts + MOE_TILE - 1) // MOE_TILE) * MOE_TILE
    gend = jnp.cumsum(gsz)
    goff = gend - gsz
    pos = goff[e_flat] + rank
    tok = jnp.tile(jnp.arange(N_TOK, dtype=jnp.int32), 2)
    src_tok = jnp.zeros((N_SLOTS,), jnp.int32).at[pos].set(tok)
    w_slot = jnp.zeros((N_SLOTS,), jnp.float32).at[pos].set(wts.reshape(-1))
    starts = jnp.arange(N_MOE_TILES, dtype=jnp.int32) * MOE_TILE
    te = jnp.sum((gend[None, :] <= starts[:, None]).astype(jnp.int32), axis=1)
    valid = (te < N_EXPERTS).astype(jnp.int32)
    last = jnp.max(jnp.where(counts > 0, jnp.arange(N_EXPERTS), 0)).astype(jnp.int32)
    te = jnp.where(valid == 1, te, last)
    return pos, src_tok, w_slot.reshape(N_SLOTS, 1), te, valid


def _final_kernel(x1_ref, ya_ref, yb_ref, mod_ref, o_ref):
    o_ref[0] = x1_ref[0] + mod_ref[0][5:6, :] * (ya_ref[0] + yb_ref[0])


def _final_call(x1, ya, yb, mod3):
    n_lat = SEQ // TOK_TILE
    lat = pl.BlockSpec((1, TOK_TILE, D_MODEL), lambda b, j: (b, j, 0))
    return pl.pallas_call(
        _final_kernel,
        out_shape=jax.ShapeDtypeStruct((BATCH, SEQ, D_MODEL), jnp.float32),
        grid=(BATCH, n_lat),
        in_specs=[lat, lat, lat, pl.BlockSpec((1, 6, D_MODEL), lambda b, j: (b, 0, 0))],
        out_specs=lat,
        name="final",
    )(x1, ya, yb, mod3)


def _block_diag2(w):
    z = jnp.zeros_like(w[0])
    return jnp.concatenate([jnp.concatenate([w[0], z], axis=1), jnp.concatenate([z, w[1]], axis=1)], axis=0)


def kernel(x, c, ctx, c_ctx, w_mod, b_mod, norm1_g, norm2_g, w_in, na_q_g, na_k_g, na_rpb, rw_mu_prev, rw_mu_next,
           rw_w0, rw_w_up, rw_a0, rw_a_up, rw_g_up, rw_k_k, rw_k_a, rw_r_k, rw_ln_g, rw_ln_b, w_out, moe_wg, moe_bg,
           moe_we, moe_be, moe_w1, moe_w3, moe_w2):
    bf = jnp.bfloat16
    mod_rows = BATCH + 8
    cs = jnp.concatenate([c, c_ctx[None, :], jnp.zeros((mod_rows - BATCH - 1, D_MODEL), jnp.float32)], axis=0)
    mod = _mod_call(cs, w_mod[0], b_mod[0][None, :])
    mod3 = mod.reshape(mod_rows, 6, D_MODEL)

    head = jnp.arange(D_RW) // HEAD_DIM
    bd = (head[:, None] == head[None, :]).astype(bf)
    qkv, p_rw = _inproj_call(x, ctx, mod3, norm1_g[0][None, :], w_in[0].astype(bf),
                             jnp.tile(na_q_g[0], H_NA)[None, :], jnp.tile(na_k_g[0], H_NA)[None, :], bd)

    na_x = _na_call(qkv, _na_bias_table(na_rpb[0]))

    cos_t, sin_t = _rope_tables()
    r, kk, v, dec, akk, kw, g, bonus = _prep_call(
        p_rw, cos_t, sin_t, rw_mu_prev[0][None, :], rw_mu_next[0][None, :],
        rw_w0[0].reshape(1, 2 * D_RW), _block_diag2(rw_w_up[0]).astype(bf),
        rw_a0[0].reshape(1, 2 * D_RW), _block_diag2(rw_a_up[0]).astype(bf),
        rw_g_up[0].astype(bf), rw_k_k[0][None, :], rw_k_a[0][None, :], rw_r_k[0].reshape(1, D_RW), bd)

    o2 = _scan_call(_to_scan_layout(r)[0], _to_scan_layout(kk)[0], _to_scan_layout(v)[0],
                    _to_scan_layout(dec), _to_scan_layout(akk), _to_scan_layout(kw))
    o_rw = _from_scan_layout(o2[0] + o2[1])

    wr_t = jnp.zeros((LOGIT_ROWS, D_MODEL), jnp.float32)
    wr_t = wr_t.at[0:MOE_GROUPS].set(moe_wg[0].T)
    wr_t = wr_t.at[8:].set(moe_we[0].transpose(0, 2, 1).reshape(N_EXPERTS, D_MODEL))
    br = jnp.zeros((LOGIT_ROWS,), jnp.float32).at[0:MOE_GROUPS].set(moe_bg[0]).at[8:].set(moe_be[0].reshape(-1))
    x1, h2, eid, wts = _finish_call(na_x, o_rw, bonus, g, x, mod3, rw_ln_g[0][None, :], rw_ln_b[0][None, :],
                                    w_out[0].astype(bf), norm2_g[0][None, :], bd, wr_t, br[:, None])

    pos, src_tok, w_slot, tile_e, tile_valid = _moe_plan(eid, wts)
    xs = jnp.take(h2.reshape(N_TOK, D_MODEL), src_tok, axis=0)
    ys = _moe_call(tile_e, tile_valid, xs, moe_w1[0], moe_w3[0], moe_w2[0], w_slot)
    ya = jnp.take(ys, pos[:N_TOK], axis=0).reshape(BATCH, SEQ, D_MODEL)
    yb = jnp.take(ys, pos[N_TOK:], axis=0).reshape(BATCH, SEQ, D_MODEL)
    return _final_call(x1, ya, yb, mod3)
```

```python
import functools
import math

import jax
import jax.numpy as jnp
import numpy as np
from jax import lax
from jax.experimental import pallas as pl
from jax.experimental.pallas import tpu as pltpu

D_MODEL = 1024
BATCH = 16
SEQ = 2048
GRID_W = 64
GRID_H = SEQ // GRID_W
CTX_LEN = 256
T_ALL = CTX_LEN + SEQ
HEAD_DIM = 64
D_NA = 512
D_RW = 512
H_NA = D_NA // HEAD_DIM
H_RW = D_RW // HEAD_DIM
NA_KH = 8
NA_KW = 16
LORA_W = 64
LORA_A = 64
LORA_G = 128
D_RW_IN = 3 * D_RW + 2 * (LORA_W + LORA_A) + LORA_G
D_IN = 3 * D_NA + D_RW_IN
MOE_GROUPS = 4
MOE_PER_GROUP = 8
N_EXPERTS = MOE_GROUPS * MOE_PER_GROUP
D_EXPERT = 512
ROPE_THETA = 10000.0
NORM_EPS = 1e-6
RW_LN_EPS = 64e-5
NEG = -1e30
DECAY_SCALE = math.exp(-0.5)

LANES = 128
TOK_TILE = 256
N_TILES = T_ALL // TOK_TILE
SCAN_CHUNK = 16
MOE_TILE = 256
N_TOK = BATCH * SEQ
N_SLOTS = 2 * N_TOK + N_EXPERTS * MOE_TILE
N_MOE_TILES = N_SLOTS // MOE_TILE
LOGIT_ROWS = 8 + N_EXPERTS
ROW_CHUNKS = D_MODEL // LANES
VMEM_LIMIT = 48 * 1024 * 1024

_HI = lax.Precision.HIGHEST


def _sigmoid(x):
    return 1.0 / (1.0 + jnp.exp(-x))


def _silu(x):
    return x * _sigmoid(x)


def _rmsnorm_rows(xf, g):
    return xf * lax.rsqrt(jnp.mean(xf * xf, axis=-1, keepdims=True) + NORM_EPS) * g


def _group_sum(x, bd):
    hi = x.astype(jnp.bfloat16)
    lo = (x - hi.astype(jnp.float32)).astype(jnp.bfloat16)
    return (jnp.dot(hi, bd, preferred_element_type=jnp.float32)
            + jnp.dot(lo, bd, preferred_element_type=jnp.float32))


def _mod_kernel(c_ref, w_ref, b_ref, o_ref):
    o_ref[...] = jnp.dot(_silu(c_ref[...]), w_ref[...], precision=_HI,
                         preferred_element_type=jnp.float32) + b_ref[...]


def _mod_call(cs, w_mod, b_mod):
    rows = cs.shape[0]
    n = w_mod.shape[1]
    blk = 1024
    return pl.pallas_call(
        _mod_kernel,
        out_shape=jax.ShapeDtypeStruct((rows, n), jnp.float32),
        grid=(n // blk,),
        in_specs=[pl.BlockSpec((rows, D_MODEL), lambda j: (0, 0)),
                  pl.BlockSpec((D_MODEL, blk), lambda j: (0, j)),
                  pl.BlockSpec((1, blk), lambda j: (0, j))],
        out_specs=pl.BlockSpec((rows, blk), lambda j: (0, j)),
        name="mod",
    )(cs, w_mod, b_mod)


def _inproj_kernel(x_ref, ctx_ref, modx_ref, modc_ref, g_ref, w_ref, qg_ref, kg_ref, bd_ref, na_ref, rw_ref):
    is_ctx = pl.program_id(1) == 0
    xin = jnp.where(is_ctx, ctx_ref[0], x_ref[0])
    mod = jnp.where(is_ctx, modc_ref[0], modx_ref[0])
    h = _rmsnorm_rows(xin, g_ref[...]) * (1.0 + mod[1:2, :]) + mod[0:1, :]
    p = jnp.dot(h.astype(jnp.bfloat16), w_ref[...], preferred_element_type=jnp.float32)
    rw_ref[0] = p[:, 3 * D_NA:]
    bd = bd_ref[...]
    q = p[:, 0:D_NA]
    k = p[:, D_NA:2 * D_NA]
    qn = q * lax.rsqrt(_group_sum(q * q, bd) * (1.0 / HEAD_DIM) + NORM_EPS) * (qg_ref[...] * HEAD_DIM ** -0.5)
    kn = k * lax.rsqrt(_group_sum(k * k, bd) * (1.0 / HEAD_DIM) + NORM_EPS) * kg_ref[...]
    na_ref[0, :, 0:D_NA] = qn.astype(jnp.bfloat16)
    na_ref[0, :, D_NA:2 * D_NA] = kn.astype(jnp.bfloat16)
    na_ref[0, :, 2 * D_NA:] = p[:, 2 * D_NA:3 * D_NA].astype(jnp.bfloat16)


def _inproj_call(x, ctx, mod3, norm_g, w_in_bf16, q_g, k_g, bd):
    return pl.pallas_call(
        _inproj_kernel,
        out_shape=(jax.ShapeDtypeStruct((BATCH, T_ALL, 3 * D_NA), jnp.bfloat16),
                   jax.ShapeDtypeStruct((BATCH, T_ALL, D_RW_IN), jnp.float32)),
        grid=(BATCH, N_TILES),
        in_specs=[pl.BlockSpec((1, TOK_TILE, D_MODEL), lambda b, j: (b, jnp.maximum(j - 1, 0), 0)),
                  pl.BlockSpec((1, TOK_TILE, D_MODEL), lambda b, j: (b, 0, 0)),
                  pl.BlockSpec((1, 6, D_MODEL), lambda b, j: (b, 0, 0)),
                  pl.BlockSpec((1, 6, D_MODEL), lambda b, j: (BATCH, 0, 0)),
                  pl.BlockSpec((1, D_MODEL), lambda b, j: (0, 0)),
                  pl.BlockSpec((D_MODEL, D_IN), lambda b, j: (0, 0)),
                  pl.BlockSpec((1, D_NA), lambda b, j: (0, 0)),
                  pl.BlockSpec((1, D_NA), lambda b, j: (0, 0)),
                  pl.BlockSpec((D_NA, D_NA), lambda b, j: (0, 0))],
        out_specs=(pl.BlockSpec((1, TOK_TILE, 3 * D_NA), lambda b, j: (b, j, 0)),
                   pl.BlockSpec((1, TOK_TILE, D_RW_IN), lambda b, j: (b, j, 0))),
        compiler_params=pltpu.CompilerParams(vmem_limit_bytes=VMEM_LIMIT),
        name="inproj",
    )(x, ctx, mod3, mod3, norm_g, w_in_bf16, q_g, k_g, bd)


NA_QROWS = 4
NA_QBLK = NA_QROWS * GRID_W
NA_BAND = NA_KH + NA_QROWS - 1
NA_BAND_KEYS = NA_BAND * GRID_W
NA_NBLK = GRID_H // NA_QROWS


def _na_band_start(i):
    return np.clip(i * NA_QROWS - NA_KH // 2, 0, GRID_H - NA_BAND)


def _na_kernel(q_ref, k_ref, v_ref, bias_ref, o_ref, qs, ks, vs):
    for h in range(2):
        hs = slice(h * HEAD_DIM, (h + 1) * HEAD_DIM)
        qs[h] = q_ref[0, CTX_LEN:, hs]
        ks[h] = k_ref[0, :, hs]
        vs[h] = v_ref[0, :, hs]

    nt = (((1,), (1,)), ((), ()))

    def body(i, carry):
        bs = jnp.clip(i * NA_QROWS - NA_KH // 2, 0, GRID_H - NA_BAND)
        pattern = jnp.where(i == 0, 0, jnp.where(i == NA_NBLK - 1, 2, 1))
        q0 = pl.multiple_of(i * NA_QBLK, NA_QBLK)
        k0 = pl.multiple_of(CTX_LEN + bs * GRID_W, GRID_W)
        for h in range(2):
            qb = qs[h, pl.ds(q0, NA_QBLK), :]
            kw = ks[h, pl.ds(k0, NA_BAND_KEYS), :]
            vw = vs[h, pl.ds(k0, NA_BAND_KEYS), :]
            kc = ks[h, 0:CTX_LEN, :]
            vc = vs[h, 0:CTX_LEN, :]
            s_win = lax.dot_general(qb, kw, nt, preferred_element_type=jnp.float32) + bias_ref[h, pattern]
            s_ctx = lax.dot_general(qb, kc, nt, preferred_element_type=jnp.float32)
            m = jnp.maximum(jnp.max(s_win, axis=-1, keepdims=True), jnp.max(s_ctx, axis=-1, keepdims=True))
            e_win = jnp.exp(s_win - m)
            e_ctx = jnp.exp(s_ctx - m)
            den = jnp.sum(e_win, axis=-1, keepdims=True) + jnp.sum(e_ctx, axis=-1, keepdims=True)
            o = (jnp.dot(e_win.astype(jnp.bfloat16), vw, preferred_element_type=jnp.float32)
                 + jnp.dot(e_ctx.astype(jnp.bfloat16), vc, preferred_element_type=jnp.float32))
            o_ref[0, pl.ds(q0, NA_QBLK), h * HEAD_DIM:(h + 1) * HEAD_DIM] = o / den
        return carry

    lax.fori_loop(0, NA_NBLK, body, 0)


def _na_call(qkv, bias8):
    n_hp = D_NA // LANES
    return pl.pallas_call(
        _na_kernel,
        out_shape=jax.ShapeDtypeStruct((BATCH, SEQ, D_NA), jnp.float32),
        grid=(n_hp, BATCH),
        in_specs=[pl.BlockSpec((1, T_ALL, LANES), lambda hp, b: (b, 0, hp)),
                  pl.BlockSpec((1, T_ALL, LANES), lambda hp, b: (b, 0, n_hp + hp)),
                  pl.BlockSpec((1, T_ALL, LANES), lambda hp, b: (b, 0, 2 * n_hp + hp)),
                  pl.BlockSpec((2, 3, NA_QBLK, NA_BAND_KEYS), lambda hp, b: (hp, 0, 0, 0))],
        out_specs=pl.BlockSpec((1, SEQ, LANES), lambda hp, b: (b, 0, hp)),
        scratch_shapes=[pltpu.VMEM((2, SEQ, HEAD_DIM), jnp.bfloat16),
                        pltpu.VMEM((2, T_ALL, HEAD_DIM), jnp.bfloat16),
                        pltpu.VMEM((2, T_ALL, HEAD_DIM), jnp.bfloat16)],
        compiler_params=pltpu.CompilerParams(vmem_limit_bytes=VMEM_LIMIT),
        name="na",
    )(qkv, qkv, qkv, bias8)


def _na_bias_table(rpb):
    qc = np.arange(GRID_W)
    cs = np.clip(qc - NA_KW // 2, 0, GRID_W - NA_KW)
    kc = np.arange(GRID_W)
    col_ok = (kc[None, :] >= cs[:, None]) & (kc[None, :] < cs[:, None] + NA_KW)
    dc = np.clip(kc[None, :] - qc[:, None], -(NA_KW - 1), NA_KW - 1) + NA_KW - 1
    tz = jnp.where(col_ok[None, None], rpb[:, :, dc], NEG)
    masked = jnp.full((H_NA, GRID_W, GRID_W), NEG, jnp.float32)
    patterns = []
    for i in (0, 1, NA_NBLK - 1):
        bs = int(_na_band_start(i))
        rows = []
        for g in range(NA_QROWS):
            qr = i * NA_QROWS + g
            rs = int(np.clip(qr - NA_KH // 2, 0, GRID_H - NA_KH))
            blocks = []
            for j in range(NA_BAND):
                kr = bs + j
                blocks.append(tz[:, kr - qr + NA_KH - 1] if rs <= kr < rs + NA_KH else masked)
            rows.append(jnp.concatenate(blocks, axis=-1))
        patterns.append(jnp.concatenate(rows, axis=-2))
    return jnp.stack(patterns, axis=1).astype(jnp.float32)


def _swap16(x):
    lane = lax.broadcasted_iota(jnp.int32, x.shape, 1)
    return jnp.where((lane & 16) == 0, pltpu.roll(x, LANES - 16, axis=1), pltpu.roll(x, 16, axis=1))


def _rope(x, cos, sin):
    blocks = []
    for i in range(x.shape[1] // LANES):
        xb = x[:, i * LANES:(i + 1) * LANES]
        blocks.append(xb * cos + _swap16(xb) * sin)
    return jnp.concatenate(blocks, axis=1)


def _prep_kernel(p_ref, pv_ref, nx_ref, cos_ref, sin_ref, mup_ref, mun_ref, w0_ref, wup_ref, a0_ref, aup_ref,
                 gup_ref, kk_ref, ka_ref, rk_ref, bd_ref,
                 r_out, kkn_out, v_out, dec_out, akk_out, kw_out, g_out, bonus_out):
    j = pl.program_id(1)
    p = p_ref[0]
    row = lax.broadcasted_iota(jnp.int32, p.shape, 0)
    prev_row = jnp.where((j == 0) | (j == 1), 0.0, pv_ref[0, 7:8, :])
    next_row = jnp.where((j == 0) | (j == N_TILES - 1), 0.0, nx_ref[0, 0:1, :])
    prev = jnp.where(row == 0, prev_row, pltpu.roll(p, 1, axis=0))
    nxt = jnp.where(row == TOK_TILE - 1, next_row, pltpu.roll(p, TOK_TILE - 1, axis=0))
    ps = p + mup_ref[...] * (prev - p) + mun_ref[...] * (nxt - p)

    cos = cos_ref[...]
    sin = sin_ref[...]
    r = _rope(ps[:, 0:D_RW], cos, sin)
    k = _rope(ps[:, D_RW:2 * D_RW], cos, sin)
    v = ps[:, 2 * D_RW:3 * D_RW]
    o3 = 3 * D_RW
    wd = ps[:, o3:o3 + 2 * LORA_W]
    ad = ps[:, o3 + 2 * LORA_W:o3 + 2 * LORA_W + 2 * LORA_A]
    gd = ps[:, o3 + 2 * LORA_W + 2 * LORA_A:]

    w_pre = w0_ref[...] + jnp.dot(jnp.tanh(wd).astype(jnp.bfloat16), wup_ref[...], preferred_element_type=jnp.float32)
    decay = jnp.exp(-DECAY_SCALE * _sigmoid(w_pre))
    a = _sigmoid(a0_ref[...] + jnp.dot(ad.astype(jnp.bfloat16), aup_ref[...], preferred_element_type=jnp.float32))
    g = jnp.dot(_sigmoid(gd).astype(jnp.bfloat16), gup_ref[...], preferred_element_type=jnp.float32)

    bd = bd_ref[...]
    kk = k * kk_ref[...]
    kk = kk / jnp.maximum(jnp.sqrt(_group_sum(kk * kk, bd)), 1e-12)
    bonus = _group_sum(r * k * rk_ref[...], bd) * v
    ka = ka_ref[...]

    r_out[0] = r
    kkn_out[0] = kk
    v_out[0] = v
    g_out[0] = g
    bonus_out[0] = bonus
    for d in range(2):
        a_d = a[:, d * D_RW:(d + 1) * D_RW]
        dec_out[0, :, d * D_RW:(d + 1) * D_RW] = decay[:, d * D_RW:(d + 1) * D_RW]
        akk_out[0, :, d * D_RW:(d + 1) * D_RW] = kk * a_d
        kw_out[0, :, d * D_RW:(d + 1) * D_RW] = k * (1.0 + (a_d - 1.0) * ka)


def _prep_call(p_rw, cos_t, sin_t, mu_prev, mu_next, w0, wup2, a0, aup2, gup, k_k, k_a, r_k, bd):
    sub = TOK_TILE // 8
    n_sub = T_ALL // 8
    full = lambda shape: pl.BlockSpec(shape, lambda b, j: (0,) * len(shape))
    tok = lambda width: pl.BlockSpec((1, TOK_TILE, width), lambda b, j: (b, j, 0))
    one = jax.ShapeDtypeStruct((BATCH, T_ALL, D_RW), jnp.float32)
    two = jax.ShapeDtypeStruct((BATCH, T_ALL, 2 * D_RW), jnp.float32)
    return pl.pallas_call(
        _prep_kernel,
        out_shape=(one, one, one, two, two, two, one, one),
        grid=(BATCH, N_TILES),
        in_specs=[tok(D_RW_IN),
                  pl.BlockSpec((1, 8, D_RW_IN), lambda b, j: (b, jnp.maximum(j * sub - 1, 0), 0)),
                  pl.BlockSpec((1, 8, D_RW_IN), lambda b, j: (b, jnp.minimum((j + 1) * sub, n_sub - 1), 0)),
                  pl.BlockSpec((TOK_TILE, LANES), lambda b, j: (j, 0)),
                  pl.BlockSpec((TOK_TILE, LANES), lambda b, j: (j, 0)),
                  full((1, D_RW_IN)), full((1, D_RW_IN)),
                  full((1, 2 * D_RW)), full((2 * LORA_W, 2 * D_RW)),
                  full((1, 2 * D_RW)), full((2 * LORA_A, 2 * D_RW)),
                  full((LORA_G, D_RW)),
                  full((1, D_RW)), full((1, D_RW)), full((1, D_RW)),
                  full((D_RW, D_RW))],
        out_specs=(tok(D_RW), tok(D_RW), tok(D_RW), tok(2 * D_RW), tok(2 * D_RW), tok(2 * D_RW), tok(D_RW), tok(D_RW)),
        compiler_params=pltpu.CompilerParams(vmem_limit_bytes=VMEM_LIMIT),
        name="prep",
    )(p_rw, p_rw, p_rw, cos_t, sin_t, mu_prev, mu_next, w0, wup2, a0, aup2, gup, k_k, k_a, r_k, bd)


def _rope_tables():
    nf = HEAD_DIM // 4
    pos = np.arange(SEQ)
    inv = ROPE_THETA ** (-np.arange(nf, dtype=np.float32) / nf)
    lane = np.arange(LANES) % HEAD_DIM
    half, pair, f = lane // 32, (lane % 32) // 16, lane % 16
    coord = np.where(half[None, :] == 0, (pos // GRID_W)[:, None], (pos % GRID_W)[:, None]).astype(np.float32)
    ang = coord * inv[f][None, :].astype(np.float32)
    cos = np.cos(ang).astype(np.float32)
    sin = np.sin(ang).astype(np.float32) * np.where(pair == 0, -1.0, 1.0)[None, :].astype(np.float32)
    cos = np.concatenate([np.ones((CTX_LEN, LANES), np.float32), cos], axis=0)
    sin = np.concatenate([np.zeros((CTX_LEN, LANES), np.float32), sin], axis=0)
    return jnp.asarray(cos), jnp.asarray(sin)


def _scan_kernel(r_ref, kk_ref, v_ref, w_ref, akk_ref, kw_ref, o_ref, s_ref):
    d = pl.program_id(0)
    i = pl.program_id(1)

    @pl.when(i == 0)
    def _():
        s_ref[...] = jnp.zeros_like(s_ref)

    k_unroll = 8
    zero = jnp.zeros((HEAD_DIM, LANES), jnp.float32)

    def make_step(with_out):
        def step(s, carry):
            t = jnp.where(d == 0, s, SCAN_CHUNK - 1 - s)
            vv = v_ref[t]

            def sk_body(kb, sk):
                for u in range(k_unroll):
                    k = kb * k_unroll + u
                    sk = sk + s_ref[k] * kk_ref[t, pl.ds(k, 1), :]
                return sk

            sk = lax.fori_loop(0, HEAD_DIM // k_unroll, sk_body, zero)

            def upd_body(kb, o):
                for u in range(k_unroll):
                    k = kb * k_unroll + u
                    s_new = (s_ref[k] * w_ref[0, t, pl.ds(k, 1), :] - sk * akk_ref[0, t, pl.ds(k, 1), :]
                             + vv * kw_ref[0, t, pl.ds(k, 1), :])
                    s_ref[k] = s_new
                    if with_out:
                        o = o + s_new * r_ref[t, pl.ds(k, 1), :]
                return o

            o = lax.fori_loop(0, HEAD_DIM // k_unroll, upd_body, zero)
            if with_out:
                o_ref[0, t] = o
            return carry
        return step

    @pl.when(i < CTX_LEN // SCAN_CHUNK)
    def _():
        lax.fori_loop(0, SCAN_CHUNK, make_step(False), 0)

    @pl.when(i >= CTX_LEN // SCAN_CHUNK)
    def _():
        lax.fori_loop(0, SCAN_CHUNK, make_step(True), 0)


def _scan_call(r_t, kk_t, v_t, w_t, akk_t, kw_t):
    nc = CTX_LEN // SCAN_CHUNK
    nl = SEQ // SCAN_CHUNK

    def blk(d, i):
        rev = jnp.where(i < nc, nc - 1 - i, 2 * nc + nl - 1 - i)
        return jnp.where(d == 0, i, rev)

    def oblk(d, i):
        fwd = jnp.maximum(i - nc, 0)
        rev = jnp.where(i < nc, nl - 1, nc + nl - 1 - i)
        return jnp.where(d == 0, fwd, rev)

    shared = pl.BlockSpec((SCAN_CHUNK, HEAD_DIM, LANES), lambda d, i: (blk(d, i), 0, 0))
    perdir = pl.BlockSpec((1, SCAN_CHUNK, HEAD_DIM, LANES), lambda d, i: (d, blk(d, i), 0, 0))
    return pl.pallas_call(
        _scan_kernel,
        out_shape=jax.ShapeDtypeStruct((2, SEQ, HEAD_DIM, LANES), jnp.float32),
        grid=(2, nc + nl),
        in_specs=[shared, shared, shared, perdir, perdir, perdir],
        out_specs=pl.BlockSpec((1, SCAN_CHUNK, HEAD_DIM, LANES), lambda d, i: (d, oblk(d, i), 0, 0)),
        scratch_shapes=[pltpu.VMEM((HEAD_DIM, HEAD_DIM, LANES), jnp.float32)],
        compiler_params=pltpu.CompilerParams(dimension_semantics=("arbitrary", "arbitrary"),
                                             vmem_limit_bytes=VMEM_LIMIT),
        name="scan",
    )(r_t, kk_t, v_t, w_t, akk_t, kw_t)


def _to_scan_layout(a):
    n = a.shape[-1] // D_RW
    a = a.reshape(BATCH, T_ALL, n, H_RW, HEAD_DIM)
    return a.transpose(2, 1, 4, 0, 3).reshape(n, T_ALL, HEAD_DIM, LANES)


def _from_scan_layout(o):
    return o.reshape(SEQ, HEAD_DIM, BATCH, H_RW).transpose(2, 0, 3, 1).reshape(BATCH, SEQ, D_RW)


def _finish_kernel(na_ref, o_ref, bonus_ref, g_ref, x_ref, mod_ref, lng_ref, lnb_ref, wout_ref, n2g_ref, bd_ref,
                   wr_ref, br_ref, uj_ref, x1_out, h2_out, eid_out, wt_out, rank_out, cnt_out, run_ref):
    @pl.when((pl.program_id(0) == 0) & (pl.program_id(1) == 0))
    def _():
        run_ref[...] = jnp.zeros_like(run_ref)

    bd = bd_ref[...]
    o = o_ref[0]
    mu = _group_sum(o, bd) * (1.0 / HEAD_DIM)
    oc = o - mu
    var = _group_sum(oc * oc, bd) * (1.0 / HEAD_DIM)
    y = oc * lax.rsqrt(var + RW_LN_EPS) * lng_ref[...] + lnb_ref[...]
    rw = (y + bonus_ref[0]) * g_ref[0]
    mix = jnp.concatenate([na_ref[0], rw], axis=-1).astype(jnp.bfloat16)
    yx = jnp.dot(mix, wout_ref[...], preferred_element_type=jnp.float32)
    mod = mod_ref[0]
    x1 = x_ref[0] + mod[2:3, :] * yx
    x1_out[0] = x1
    h2 = _rmsnorm_rows(x1, n2g_ref[...]) * (1.0 + mod[4:5, :]) + mod[3:4, :]
    for c in range(ROW_CHUNKS):
        h2_out[pl.ds(c, TOK_TILE, stride=ROW_CHUNKS), :] = h2[:, c * LANES:(c + 1) * LANES]

    lg = lax.dot_general(wr_ref[...], h2, (((1,), (1,)), ((), ())), precision=_HI,
                         preferred_element_type=jnp.float32) + br_ref[...]
    gl = lg[0:MOE_GROUPS, :]
    grow = lax.broadcasted_iota(jnp.int32, gl.shape, 0)
    gmax = jnp.max(gl, axis=0, keepdims=True)
    g_sel = jnp.min(jnp.where(gl == gmax, grow, MOE_GROUPS), axis=0, keepdims=True)
    p_group = 1.0 / jnp.sum(jnp.exp(gl - gmax), axis=0, keepdims=True)
    el = jnp.zeros((MOE_PER_GROUP, TOK_TILE), jnp.float32)
    for gi in range(MOE_GROUPS):
        el = jnp.where(g_sel == gi, lg[8 + gi * MOE_PER_GROUP:8 + (gi + 1) * MOE_PER_GROUP, :], el)
    erow = lax.broadcasted_iota(jnp.int32, el.shape, 0)
    m1 = jnp.max(el, axis=0, keepdims=True)
    i1 = jnp.min(jnp.where(el == m1, erow, MOE_PER_GROUP), axis=0, keepdims=True)
    el2 = jnp.where(erow == i1, -jnp.inf, el)
    m2 = jnp.max(el2, axis=0, keepdims=True)
    i2 = jnp.min(jnp.where(el2 == m2, erow, MOE_PER_GROUP), axis=0, keepdims=True)
    e21 = jnp.exp(m2 - m1)
    e0 = g_sel * MOE_PER_GROUP + i1
    e1 = g_sel * MOE_PER_GROUP + i2
    eid_out[0:1, :] = e0
    eid_out[1:2, :] = e1
    wt_out[0:1, :] = p_group / (1.0 + e21)
    wt_out[1:2, :] = p_group * e21 / (1.0 + e21)

    xrow = lax.broadcasted_iota(jnp.int32, (N_EXPERTS, TOK_TILE), 0)
    m0 = xrow == e0
    m1b = xrow == e1
    c0 = jnp.dot(jnp.where(m0, 1.0, 0.0).astype(jnp.bfloat16), uj_ref[...], preferred_element_type=jnp.float32)
    c1 = jnp.dot(jnp.where(m1b, 1.0, 0.0).astype(jnp.bfloat16), uj_ref[...], preferred_element_type=jnp.float32)
    run = run_ref[...]
    tot0 = c0[:, TOK_TILE:]
    rank_out[0:1, :] = jnp.sum(jnp.where(m0, run + c0[:, :TOK_TILE], 0.0), axis=0, keepdims=True).astype(jnp.int32)
    rank_out[1:2, :] = jnp.sum(jnp.where(m1b, run + tot0 + c1[:, :TOK_TILE], 0.0), axis=0,
                               keepdims=True).astype(jnp.int32)
    run = run + tot0 + c1[:, TOK_TILE:]
    run_ref[...] = run
    cnt_out[...] = run


def _finish_call(na_x, o_rw, bonus, g, x, mod3, ln_g, ln_b, w_out_bf16, norm2_g, bd, wr_t, br, uj):
    n_lat = SEQ // TOK_TILE
    full = lambda shape: pl.BlockSpec(shape, lambda b, j: (0,) * len(shape))
    lat = lambda width: pl.BlockSpec((1, TOK_TILE, width), lambda b, j: (b, j, 0))
    allt = lambda width: pl.BlockSpec((1, TOK_TILE, width), lambda b, j: (b, j + 1, 0))
    tokl = pl.BlockSpec((2, TOK_TILE), lambda b, j: (0, b * n_lat + j))
    return pl.pallas_call(
        _finish_kernel,
        out_shape=(jax.ShapeDtypeStruct((BATCH, SEQ, D_MODEL), jnp.float32),
                   jax.ShapeDtypeStruct((N_TOK * ROW_CHUNKS, LANES), jnp.float32),
                   jax.ShapeDtypeStruct((2, N_TOK), jnp.int32),
                   jax.ShapeDtypeStruct((2, N_TOK), jnp.float32),
                   jax.ShapeDtypeStruct((2, N_TOK), jnp.int32),
                   jax.ShapeDtypeStruct((N_EXPERTS, TOK_TILE), jnp.float32)),
        grid=(BATCH, n_lat),
        in_specs=[lat(D_NA), lat(D_RW), allt(D_RW), allt(D_RW), lat(D_MODEL),
                  pl.BlockSpec((1, 6, D_MODEL), lambda b, j: (b, 0, 0)),
                  full((1, D_RW)), full((1, D_RW)), full((D_MODEL, D_MODEL)), full((1, D_MODEL)),
                  full((D_RW, D_RW)), full((LOGIT_ROWS, D_MODEL)), full((LOGIT_ROWS, 1)),
                  full((TOK_TILE, 2 * TOK_TILE))],
        out_specs=(lat(D_MODEL), pl.BlockSpec((TOK_TILE * ROW_CHUNKS, LANES), lambda b, j: (b * n_lat + j, 0)),
                   tokl, tokl, tokl, full((N_EXPERTS, TOK_TILE))),
        scratch_shapes=[pltpu.VMEM((N_EXPERTS, TOK_TILE), jnp.float32)],
        compiler_params=pltpu.CompilerParams(dimension_semantics=("arbitrary", "arbitrary"),
                                             vmem_limit_bytes=VMEM_LIMIT),
        name="finish",
    )(na_x, o_rw, bonus, g, x, mod3, ln_g, ln_b, w_out_bf16, norm2_g, bd, wr_t, br, uj)


def _row_copy(h2_hbm, xs_hbm, src_row, dst_row, sem):
    return pltpu.make_async_copy(h2_hbm.at[src_row], xs_hbm.at[dst_row], sem)


def _zero_copy(zeros_hbm, xs_hbm, gend_ref, e, sem):
    return pltpu.make_async_copy(zeros_hbm, xs_hbm.at[pl.ds(gend_ref[e] - MOE_TILE, MOE_TILE)], sem)


def _dispatch_kernel(gend_ref, gsz_ref, pos_ref, h2_hbm, zeros_hbm, xs_hbm, sem_zero, sem_row):
    s = pl.program_id(0)
    picks = 2 * TOK_TILE

    @pl.when(s == 0)
    def _():
        for e in range(N_EXPERTS):
            @pl.when(gsz_ref[e] > 0)
            def _():
                _zero_copy(zeros_hbm, xs_hbm, gend_ref, e, sem_zero).start()
        for e in range(N_EXPERTS):
            @pl.when(gsz_ref[e] > 0)
            def _():
                _zero_copy(zeros_hbm, xs_hbm, gend_ref, e, sem_zero).wait()

    base = s * TOK_TILE

    def issue(i, carry):
        _row_copy(h2_hbm, xs_hbm, base + i, pos_ref[0, i], sem_row).start()
        _row_copy(h2_hbm, xs_hbm, base + i, pos_ref[1, i], sem_row).start()
        return carry

    lax.fori_loop(0, TOK_TILE, issue, 0, unroll=8)

    def drain(i, carry):
        _row_copy(h2_hbm, xs_hbm, 0, 0, sem_row).wait()
        return carry

    @pl.when(s > 0)
    def _():
        lax.fori_loop(0, picks, drain, 0, unroll=8)

    @pl.when(s == pl.num_programs(0) - 1)
    def _():
        lax.fori_loop(0, picks, drain, 0, unroll=8)


def _dispatch_call(gend, gsz, pos, h2p, zeros_tile):
    return pl.pallas_call(
        _dispatch_kernel,
        out_shape=jax.ShapeDtypeStruct((N_SLOTS, ROW_CHUNKS, LANES), jnp.float32),
        grid_spec=pltpu.PrefetchScalarGridSpec(
            num_scalar_prefetch=2,
            grid=(N_TOK // TOK_TILE,),
            in_specs=[pl.BlockSpec((2, TOK_TILE), lambda s, ge, gs: (0, s), memory_space=pltpu.SMEM),
                      pl.BlockSpec(memory_space=pl.ANY),
                      pl.BlockSpec(memory_space=pl.ANY)],
            out_specs=pl.BlockSpec(memory_space=pl.ANY),
            scratch_shapes=[pltpu.SemaphoreType.DMA, pltpu.SemaphoreType.DMA]),
        compiler_params=pltpu.CompilerParams(dimension_semantics=("arbitrary",)),
        name="dispatch",
    )(gend, gsz, pos, h2p, zeros_tile)


def _moe_kernel(te_ref, tv_ref, tx_ref, xs_ref, w1_ref, w3_ref, w2_ref, y_ref, w1s, w3s, w2s):
    t = pl.program_id(0)

    @pl.when((t == 0) | (te_ref[t] != te_ref[jnp.maximum(t - 1, 0)]))
    def _():
        w1s[...] = w1_ref[0].astype(jnp.bfloat16)
        w3s[...] = w3_ref[0].astype(jnp.bfloat16)
        w2s[...] = w2_ref[0].astype(jnp.bfloat16)

    @pl.when(tv_ref[t] == 1)
    def _():
        xs = jnp.concatenate([xs_ref[pl.ds(c, MOE_TILE, stride=ROW_CHUNKS), :].astype(jnp.bfloat16)
                              for c in range(ROW_CHUNKS)], axis=1)
        h1 = jnp.dot(xs, w1s[...], preferred_element_type=jnp.float32)
        h3 = jnp.dot(xs, w3s[...], preferred_element_type=jnp.float32)
        he = (_silu(h1) * h3).astype(jnp.bfloat16)
        y_ref[...] = jnp.dot(he, w2s[...], preferred_element_type=jnp.float32)

    @pl.when(tv_ref[t] == 0)
    def _():
        y_ref[...] = jnp.zeros_like(y_ref)


def _moe_call(tile_e, tile_valid, tile_src, xs, w1, w3, w2):
    return pl.pallas_call(
        _moe_kernel,
        out_shape=jax.ShapeDtypeStruct((N_SLOTS, D_MODEL), jnp.float32),
        grid_spec=pltpu.PrefetchScalarGridSpec(
            num_scalar_prefetch=3,
            grid=(N_MOE_TILES,),
            in_specs=[pl.BlockSpec((MOE_TILE * ROW_CHUNKS, LANES), lambda t, te, tv, tx: (tx[t], 0)),
                      pl.BlockSpec((1, D_MODEL, D_EXPERT), lambda t, te, tv, tx: (te[t], 0, 0)),
                      pl.BlockSpec((1, D_MODEL, D_EXPERT), lambda t, te, tv, tx: (te[t], 0, 0)),
                      pl.BlockSpec((1, D_EXPERT, D_MODEL), lambda t, te, tv, tx: (te[t], 0, 0))],
            out_specs=pl.BlockSpec((MOE_TILE, D_MODEL), lambda t, te, tv, tx: (t, 0)),
            scratch_shapes=[pltpu.VMEM((D_MODEL, D_EXPERT), jnp.bfloat16),
                            pltpu.VMEM((D_MODEL, D_EXPERT), jnp.bfloat16),
                            pltpu.VMEM((D_EXPERT, D_MODEL), jnp.bfloat16)]),
        compiler_params=pltpu.CompilerParams(dimension_semantics=("arbitrary",), vmem_limit_bytes=VMEM_LIMIT),
        name="moe",
    )(tile_e, tile_valid, tile_src, xs, w1, w3, w2)


def _moe_plan(eid, rank, cnt):
    counts = cnt[:, 0].astype(jnp.int32)
    gsz = ((counts + MOE_TILE - 1) // MOE_TILE) * MOE_TILE
    gend = jnp.cumsum(gsz)
    goff = gend - gsz
    pos = rank
    for e in range(N_EXPERTS):
        pos = pos + jnp.where(eid == e, goff[e], 0)
    starts = jnp.arange(N_MOE_TILES, dtype=jnp.int32) * MOE_TILE
    te = jnp.sum((gend[None, :] <= starts[:, None]).astype(jnp.int32), axis=1)
    valid = (te < N_EXPERTS).astype(jnp.int32)
    last = jnp.max(jnp.where(counts > 0, jnp.arange(N_EXPERTS), 0)).astype(jnp.int32)
    te = jnp.where(valid == 1, te, last)
    n_valid = gend[-1] // MOE_TILE
    tx = jnp.minimum(jnp.arange(N_MOE_TILES, dtype=jnp.int32), n_valid - 1)
    return pos, gend.astype(jnp.int32), gsz.astype(jnp.int32), te, valid, tx


def _final_kernel(x1_ref, ya_ref, yb_ref, wt_ref, mod_ref, o_ref):
    w = wt_ref[...]
    o_ref[0] = x1_ref[0] + mod_ref[0][5:6, :] * (w[:, 0:1] * ya_ref[0] + w[:, 1:2] * yb_ref[0])


def _final_call(x1, ya, yb, wt_rows, mod3):
    n_lat = SEQ // TOK_TILE
    lat = pl.BlockSpec((1, TOK_TILE, D_MODEL), lambda b, j: (b, j, 0))
    return pl.pallas_call(
        _final_kernel,
        out_shape=jax.ShapeDtypeStruct((BATCH, SEQ, D_MODEL), jnp.float32),
        grid=(BATCH, n_lat),
        in_specs=[lat, lat, lat, pl.BlockSpec((TOK_TILE, 2), lambda b, j: (b * n_lat + j, 0)),
                  pl.BlockSpec((1, 6, D_MODEL), lambda b, j: (b, 0, 0))],
        out_specs=lat,
        name="final",
    )(x1, ya, yb, wt_rows, mod3)


def _block_diag2(w):
    z = jnp.zeros_like(w[0])
    return jnp.concatenate([jnp.concatenate([w[0], z], axis=1), jnp.concatenate([z, w[1]], axis=1)], axis=0)


def kernel(x, c, ctx, c_ctx, w_mod, b_mod, norm1_g, norm2_g, w_in, na_q_g, na_k_g, na_rpb, rw_mu_prev, rw_mu_next,
           rw_w0, rw_w_up, rw_a0, rw_a_up, rw_g_up, rw_k_k, rw_k_a, rw_r_k, rw_ln_g, rw_ln_b, w_out, moe_wg, moe_bg,
           moe_we, moe_be, moe_w1, moe_w3, moe_w2):
    bf = jnp.bfloat16
    mod_rows = BATCH + 8
    cs = jnp.concatenate([c, c_ctx[None, :], jnp.zeros((mod_rows - BATCH - 1, D_MODEL), jnp.float32)], axis=0)
    mod = _mod_call(cs, w_mod[0], b_mod[0][None, :])
    mod3 = mod.reshape(mod_rows, 6, D_MODEL)

    head = jnp.arange(D_RW) // HEAD_DIM
    bd = (head[:, None] == head[None, :]).astype(bf)
    qkv, p_rw = _inproj_call(x, ctx, mod3, norm1_g[0][None, :], w_in[0].astype(bf),
                             jnp.tile(na_q_g[0], H_NA)[None, :], jnp.tile(na_k_g[0], H_NA)[None, :], bd)

    na_x = _na_call(qkv, _na_bias_table(na_rpb[0]))

    cos_t, sin_t = _rope_tables()
    r, kk, v, dec, akk, kw, g, bonus = _prep_call(
        p_rw, cos_t, sin_t, rw_mu_prev[0][None, :], rw_mu_next[0][None, :],
        rw_w0[0].reshape(1, 2 * D_RW), _block_diag2(rw_w_up[0]).astype(bf),
        rw_a0[0].reshape(1, 2 * D_RW), _block_diag2(rw_a_up[0]).astype(bf),
        rw_g_up[0].astype(bf), rw_k_k[0][None, :], rw_k_a[0][None, :], rw_r_k[0].reshape(1, D_RW), bd)

    o2 = _scan_call(_to_scan_layout(r)[0], _to_scan_layout(kk)[0], _to_scan_layout(v)[0],
                    _to_scan_layout(dec), _to_scan_layout(akk), _to_scan_layout(kw))
    o_rw = _from_scan_layout(o2[0] + o2[1])

    wr_t = jnp.zeros((LOGIT_ROWS, D_MODEL), jnp.float32)
    wr_t = wr_t.at[0:MOE_GROUPS].set(moe_wg[0].T)
    wr_t = wr_t.at[8:].set(moe_we[0].transpose(0, 2, 1).reshape(N_EXPERTS, D_MODEL))
    br = jnp.zeros((LOGIT_ROWS,), jnp.float32).at[0:MOE_GROUPS].set(moe_bg[0]).at[8:].set(moe_be[0].reshape(-1))
    tri = jnp.arange(TOK_TILE)
    uj = jnp.concatenate([(tri[:, None] < tri[None, :]).astype(bf), jnp.ones((TOK_TILE, TOK_TILE), bf)], axis=1)
    x1, h2p, eid, wts, rank, cnt = _finish_call(na_x, o_rw, bonus, g, x, mod3, rw_ln_g[0][None, :],
                                                rw_ln_b[0][None, :], w_out[0].astype(bf), norm2_g[0][None, :], bd,
                                                wr_t, br[:, None], uj)

    pos, gend, gsz, tile_e, tile_valid, tile_src = _moe_plan(eid, rank, cnt)
    xs = _dispatch_call(gend, gsz, pos, h2p.reshape(N_TOK, ROW_CHUNKS, LANES),
                        jnp.zeros((MOE_TILE, ROW_CHUNKS, LANES), jnp.float32))
    ys = _moe_call(tile_e, tile_valid, tile_src, xs.reshape(N_SLOTS * ROW_CHUNKS, LANES),
                   moe_w1[0], moe_w3[0], moe_w2[0])
    ya = jnp.take(ys, pos[0], axis=0).reshape(BATCH, SEQ, D_MODEL)
    yb = jnp.take(ys, pos[1], axis=0).reshape(BATCH, SEQ, D_MODEL)
    return _final_call(x1, ya, yb, wts.T, mod3)
```

```python
import functools
import math

import jax
import jax.numpy as jnp
import numpy as np
from jax import lax
from jax.experimental import pallas as pl
from jax.experimental.pallas import tpu as pltpu

D_MODEL = 1024
BATCH = 16
SEQ = 2048
GRID_W = 64
GRID_H = SEQ // GRID_W
CTX_LEN = 256
T_ALL = CTX_LEN + SEQ
HEAD_DIM = 64
D_NA = 512
D_RW = 512
H_NA = D_NA // HEAD_DIM
H_RW = D_RW // HEAD_DIM
NA_KH = 8
NA_KW = 16
LORA_W = 64
LORA_A = 64
LORA_G = 128
D_RW_IN = 3 * D_RW + 2 * (LORA_W + LORA_A) + LORA_G
D_IN = 3 * D_NA + D_RW_IN
MOE_GROUPS = 4
MOE_PER_GROUP = 8
N_EXPERTS = MOE_GROUPS * MOE_PER_GROUP
D_EXPERT = 512
ROPE_THETA = 10000.0
NORM_EPS = 1e-6
RW_LN_EPS = 64e-5
NEG = -1e30
DECAY_SCALE = math.exp(-0.5)

LANES = 128
TOK_TILE = 256
N_TILES = T_ALL // TOK_TILE
SCAN_CHUNK = 16
MOE_TILE = 256
N_TOK = BATCH * SEQ
N_SLOTS = 2 * N_TOK + N_EXPERTS * MOE_TILE
N_MOE_TILES = N_SLOTS // MOE_TILE
LOGIT_ROWS = 8 + N_EXPERTS
VMEM_LIMIT = 48 * 1024 * 1024

_HI = lax.Precision.HIGHEST


def _sigmoid(x):
    return 1.0 / (1.0 + jnp.exp(-x))


def _silu(x):
    return x * _sigmoid(x)


def _rmsnorm_rows(xf, g):
    return xf * lax.rsqrt(jnp.mean(xf * xf, axis=-1, keepdims=True) + NORM_EPS) * g


def _group_sum(x, bd):
    hi = x.astype(jnp.bfloat16)
    lo = (x - hi.astype(jnp.float32)).astype(jnp.bfloat16)
    return (jnp.dot(hi, bd, preferred_element_type=jnp.float32)
            + jnp.dot(lo, bd, preferred_element_type=jnp.float32))


def _mod_kernel(c_ref, w_ref, b_ref, o_ref):
    o_ref[...] = jnp.dot(_silu(c_ref[...]), w_ref[...], precision=_HI,
                         preferred_element_type=jnp.float32) + b_ref[...]


def _mod_call(cs, w_mod, b_mod):
    rows = cs.shape[0]
    n = w_mod.shape[1]
    blk = 1024
    return pl.pallas_call(
        _mod_kernel,
        out_shape=jax.ShapeDtypeStruct((rows, n), jnp.float32),
        grid=(n // blk,),
        in_specs=[pl.BlockSpec((rows, D_MODEL), lambda j: (0, 0)),
                  pl.BlockSpec((D_MODEL, blk), lambda j: (0, j)),
                  pl.BlockSpec((1, blk), lambda j: (0, j))],
        out_specs=pl.BlockSpec((rows, blk), lambda j: (0, j)),
        name="mod",
    )(cs, w_mod, b_mod)


def _inproj_kernel(x_ref, ctx_ref, modx_ref, modc_ref, g_ref, w_ref, qg_ref, kg_ref, bd_ref, na_ref, rw_ref):
    is_ctx = pl.program_id(1) == 0
    xin = jnp.where(is_ctx, ctx_ref[0], x_ref[0])
    mod = jnp.where(is_ctx, modc_ref[0], modx_ref[0])
    h = _rmsnorm_rows(xin, g_ref[...]) * (1.0 + mod[1:2, :]) + mod[0:1, :]
    p = jnp.dot(h.astype(jnp.bfloat16), w_ref[...], preferred_element_type=jnp.float32)
    rw_ref[0] = p[:, 3 * D_NA:]
    bd = bd_ref[...]
    q = p[:, 0:D_NA]
    k = p[:, D_NA:2 * D_NA]
    qn = q * lax.rsqrt(_group_sum(q * q, bd) * (1.0 / HEAD_DIM) + NORM_EPS) * (qg_ref[...] * HEAD_DIM ** -0.5)
    kn = k * lax.rsqrt(_group_sum(k * k, bd) * (1.0 / HEAD_DIM) + NORM_EPS) * kg_ref[...]
    na_ref[0, :, 0:D_NA] = qn.astype(jnp.bfloat16)
    na_ref[0, :, D_NA:2 * D_NA] = kn.astype(jnp.bfloat16)
    na_ref[0, :, 2 * D_NA:] = p[:, 2 * D_NA:3 * D_NA].astype(jnp.bfloat16)


def _inproj_call(x, ctx, mod3, norm_g, w_in_bf16, q_g, k_g, bd):
    return pl.pallas_call(
        _inproj_kernel,
        out_shape=(jax.ShapeDtypeStruct((BATCH, T_ALL, 3 * D_NA), jnp.bfloat16),
                   jax.ShapeDtypeStruct((BATCH, T_ALL, D_RW_IN), jnp.float32)),
        grid=(BATCH, N_TILES),
        in_specs=[pl.BlockSpec((1, TOK_TILE, D_MODEL), lambda b, j: (b, jnp.maximum(j - 1, 0), 0)),
                  pl.BlockSpec((1, TOK_TILE, D_MODEL), lambda b, j: (b, 0, 0)),
                  pl.BlockSpec((1, 6, D_MODEL), lambda b, j: (b, 0, 0)),
                  pl.BlockSpec((1, 6, D_MODEL), lambda b, j: (BATCH, 0, 0)),
                  pl.BlockSpec((1, D_MODEL), lambda b, j: (0, 0)),
                  pl.BlockSpec((D_MODEL, D_IN), lambda b, j: (0, 0)),
                  pl.BlockSpec((1, D_NA), lambda b, j: (0, 0)),
                  pl.BlockSpec((1, D_NA), lambda b, j: (0, 0)),
                  pl.BlockSpec((D_NA, D_NA), lambda b, j: (0, 0))],
        out_specs=(pl.BlockSpec((1, TOK_TILE, 3 * D_NA), lambda b, j: (b, j, 0)),
                   pl.BlockSpec((1, TOK_TILE, D_RW_IN), lambda b, j: (b, j, 0))),
        compiler_params=pltpu.CompilerParams(vmem_limit_bytes=VMEM_LIMIT),
        name="inproj",
    )(x, ctx, mod3, mod3, norm_g, w_in_bf16, q_g, k_g, bd)


NA_QROWS = 4
NA_QBLK = NA_QROWS * GRID_W
NA_BAND = NA_KH + NA_QROWS - 1
NA_BAND_KEYS = NA_BAND * GRID_W
NA_NBLK = GRID_H // NA_QROWS


def _na_band_start(i):
    return np.clip(i * NA_QROWS - NA_KH // 2, 0, GRID_H - NA_BAND)


def _na_kernel(q_ref, k_ref, v_ref, tz_ref, o_ref, qs, ks, vs, bias_ref):
    @pl.when(pl.program_id(1) == 0)
    def _():
        masked = jnp.full((GRID_W, GRID_W), NEG, jnp.float32)
        for h in range(2):
            for p, i in enumerate((0, 1, NA_NBLK - 1)):
                bs = int(_na_band_start(i))
                for g in range(NA_QROWS):
                    qr = i * NA_QROWS + g
                    rs = int(np.clip(qr - NA_KH // 2, 0, GRID_H - NA_KH))
                    for j in range(NA_BAND):
                        kr = bs + j
                        blk = tz_ref[h, kr - qr + NA_KH - 1] if rs <= kr < rs + NA_KH else masked
                        bias_ref[h, p, g * GRID_W:(g + 1) * GRID_W, j * GRID_W:(j + 1) * GRID_W] = blk

    for h in range(2):
        hs = slice(h * HEAD_DIM, (h + 1) * HEAD_DIM)
        qs[h] = q_ref[0, CTX_LEN:, hs]
        ks[h] = k_ref[0, :, hs]
        vs[h] = v_ref[0, :, hs]

    nt = (((1,), (1,)), ((), ()))

    def body(i, carry):
        bs = jnp.clip(i * NA_QROWS - NA_KH // 2, 0, GRID_H - NA_BAND)
        pattern = jnp.where(i == 0, 0, jnp.where(i == NA_NBLK - 1, 2, 1))
        q0 = pl.multiple_of(i * NA_QBLK, NA_QBLK)
        k0 = pl.multiple_of(CTX_LEN + bs * GRID_W, GRID_W)
        for h in range(2):
            qb = qs[h, pl.ds(q0, NA_QBLK), :]
            kw = ks[h, pl.ds(k0, NA_BAND_KEYS), :]
            vw = vs[h, pl.ds(k0, NA_BAND_KEYS), :]
            kc = ks[h, 0:CTX_LEN, :]
            vc = vs[h, 0:CTX_LEN, :]
            s_win = lax.dot_general(qb, kw, nt, preferred_element_type=jnp.float32) + bias_ref[h, pattern]
            s_ctx = lax.dot_general(qb, kc, nt, preferred_element_type=jnp.float32)
            m = jnp.maximum(jnp.max(s_win, axis=-1, keepdims=True), jnp.max(s_ctx, axis=-1, keepdims=True))
            e_win = jnp.exp(s_win - m)
            e_ctx = jnp.exp(s_ctx - m)
            den = jnp.sum(e_win, axis=-1, keepdims=True) + jnp.sum(e_ctx, axis=-1, keepdims=True)
            o = (jnp.dot(e_win.astype(jnp.bfloat16), vw, preferred_element_type=jnp.float32)
                 + jnp.dot(e_ctx.astype(jnp.bfloat16), vc, preferred_element_type=jnp.float32))
            o_ref[0, pl.ds(q0, NA_QBLK), h * HEAD_DIM:(h + 1) * HEAD_DIM] = o / den
        return carry

    lax.fori_loop(0, NA_NBLK, body, 0)


def _na_call(qkv, bias8):
    n_hp = D_NA // LANES
    return pl.pallas_call(
        _na_kernel,
        out_shape=jax.ShapeDtypeStruct((BATCH, SEQ, D_NA), jnp.float32),
        grid=(n_hp, BATCH),
        in_specs=[pl.BlockSpec((1, T_ALL, LANES), lambda hp, b: (b, 0, hp)),
                  pl.BlockSpec((1, T_ALL, LANES), lambda hp, b: (b, 0, n_hp + hp)),
                  pl.BlockSpec((1, T_ALL, LANES), lambda hp, b: (b, 0, 2 * n_hp + hp)),
                  pl.BlockSpec((2, 2 * NA_KH - 1, GRID_W, GRID_W), lambda hp, b: (hp, 0, 0, 0))],
        out_specs=pl.BlockSpec((1, SEQ, LANES), lambda hp, b: (b, 0, hp)),
        scratch_shapes=[pltpu.VMEM((2, SEQ, HEAD_DIM), jnp.bfloat16),
                        pltpu.VMEM((2, T_ALL, HEAD_DIM), jnp.bfloat16),
                        pltpu.VMEM((2, T_ALL, HEAD_DIM), jnp.bfloat16),
                        pltpu.VMEM((2, 3, NA_QBLK, NA_BAND_KEYS), jnp.float32)],
        compiler_params=pltpu.CompilerParams(dimension_semantics=("arbitrary", "arbitrary"),
                                             vmem_limit_bytes=VMEM_LIMIT),
        name="na",
    )(qkv, qkv, qkv, bias8)


def _na_bias_table(rpb):
    qc = np.arange(GRID_W)
    cs = np.clip(qc - NA_KW // 2, 0, GRID_W - NA_KW)
    kc = np.arange(GRID_W)
    col_ok = (kc[None, :] >= cs[:, None]) & (kc[None, :] < cs[:, None] + NA_KW)
    dc = np.clip(kc[None, :] - qc[:, None], -(NA_KW - 1), NA_KW - 1) + NA_KW - 1
    return jnp.where(col_ok[None, None], rpb[:, :, dc], NEG).astype(jnp.float32)


def _swap16(x):
    lane = lax.broadcasted_iota(jnp.int32, x.shape, 1)
    return jnp.where((lane & 16) == 0, pltpu.roll(x, LANES - 16, axis=1), pltpu.roll(x, 16, axis=1))


def _rope(x, cos, sin):
    blocks = []
    for i in range(x.shape[1] // LANES):
        xb = x[:, i * LANES:(i + 1) * LANES]
        blocks.append(xb * cos + _swap16(xb) * sin)
    return jnp.concatenate(blocks, axis=1)


def _chunk_cumprod(w, reverse):
    n = w.shape[0]
    pos = lax.broadcasted_iota(jnp.int32, w.shape, 0) % SCAN_CHUNK
    if reverse:
        pos = SCAN_CHUNK - 1 - pos
    inc = w
    s = 1
    while s < SCAN_CHUNK:
        shifted = pltpu.roll(inc, (n - s) if reverse else s, axis=0)
        inc = inc * jnp.where(pos >= s, shifted, 1.0)
        s *= 2
    exc = jnp.where(pos >= 1, pltpu.roll(inc, (n - 1) if reverse else 1, axis=0), 1.0)
    return inc, exc


def _prep_kernel(p_ref, pv_ref, nx_ref, cos_ref, sin_ref, mup_ref, mun_ref, w0_ref, wup_ref, a0_ref, aup_ref,
                 gup_ref, kk_ref, ka_ref, rk_ref, bd_ref,
                 v_out, abar_out, rbar_out, ktil_out, btil_out, gend_out, g_out, bonus_out, gam_ref):
    j = pl.program_id(1)
    p = p_ref[0]
    row = lax.broadcasted_iota(jnp.int32, p.shape, 0)
    prev_row = jnp.where((j == 0) | (j == 1), 0.0, pv_ref[0, 7:8, :])
    next_row = jnp.where((j == 0) | (j == N_TILES - 1), 0.0, nx_ref[0, 0:1, :])
    prev = jnp.where(row == 0, prev_row, pltpu.roll(p, 1, axis=0))
    nxt = jnp.where(row == TOK_TILE - 1, next_row, pltpu.roll(p, TOK_TILE - 1, axis=0))
    ps = p + mup_ref[...] * (prev - p) + mun_ref[...] * (nxt - p)

    cos = cos_ref[...]
    sin = sin_ref[...]
    r = _rope(ps[:, 0:D_RW], cos, sin)
    k = _rope(ps[:, D_RW:2 * D_RW], cos, sin)
    v = ps[:, 2 * D_RW:3 * D_RW]
    o3 = 3 * D_RW
    wd = ps[:, o3:o3 + 2 * LORA_W]
    ad = ps[:, o3 + 2 * LORA_W:o3 + 2 * LORA_W + 2 * LORA_A]
    gd = ps[:, o3 + 2 * LORA_W + 2 * LORA_A:]

    w_pre = w0_ref[...] + jnp.dot(jnp.tanh(wd).astype(jnp.bfloat16), wup_ref[...], preferred_element_type=jnp.float32)
    decay = jnp.exp(-DECAY_SCALE * _sigmoid(w_pre))
    a = _sigmoid(a0_ref[...] + jnp.dot(ad.astype(jnp.bfloat16), aup_ref[...], preferred_element_type=jnp.float32))
    g = jnp.dot(_sigmoid(gd).astype(jnp.bfloat16), gup_ref[...], preferred_element_type=jnp.float32)

    bd = bd_ref[...]
    kk = k * kk_ref[...]
    kk = kk / jnp.maximum(jnp.sqrt(_group_sum(kk * kk, bd)), 1e-12)
    bonus = _group_sum(r * k * rk_ref[...], bd) * v
    ka = ka_ref[...]

    v_out[0] = v
    g_out[0] = g
    bonus_out[0] = bonus
    for d in range(2):
        ds_ = slice(d * D_RW, (d + 1) * D_RW)
        a_d = a[:, ds_]
        gam, gam_excl = _chunk_cumprod(decay[:, ds_], reverse=(d == 1))
        inv = 1.0 / gam
        abar_out[0, :, ds_] = gam_excl * kk
        rbar_out[0, :, ds_] = gam * r
        ktil_out[0, :, ds_] = k * (1.0 + (a_d - 1.0) * ka) * inv
        btil_out[0, :, ds_] = kk * a_d * inv
        n_seg = TOK_TILE // SCAN_CHUNK
        last = 0 if d == 1 else SCAN_CHUNK - 1
        for cb in range(D_RW // LANES):
            gam_ref[...] = gam[:, cb * LANES:(cb + 1) * LANES]
            gend_out[0, :, d * D_RW + cb * LANES:d * D_RW + (cb + 1) * LANES] = (
                gam_ref[pl.ds(last, n_seg, stride=SCAN_CHUNK), :])


def _prep_call(p_rw, cos_t, sin_t, mu_prev, mu_next, w0, wup2, a0, aup2, gup, k_k, k_a, r_k, bd):
    sub = TOK_TILE // 8
    n_sub = T_ALL // 8
    full = lambda shape: pl.BlockSpec(shape, lambda b, j: (0,) * len(shape))
    tok = lambda width: pl.BlockSpec((1, TOK_TILE, width), lambda b, j: (b, j, 0))
    one = jax.ShapeDtypeStruct((BATCH, T_ALL, D_RW), jnp.float32)
    two = jax.ShapeDtypeStruct((BATCH, T_ALL, 2 * D_RW), jnp.float32)
    n_seg = TOK_TILE // SCAN_CHUNK
    gend = jax.ShapeDtypeStruct((BATCH, T_ALL // SCAN_CHUNK, 2 * D_RW), jnp.float32)
    return pl.pallas_call(
        _prep_kernel,
        out_shape=(one, two, two, two, two, gend, one, one),
        grid=(BATCH, N_TILES),
        in_specs=[tok(D_RW_IN),
                  pl.BlockSpec((1, 8, D_RW_IN), lambda b, j: (b, jnp.maximum(j * sub - 1, 0), 0)),
                  pl.BlockSpec((1, 8, D_RW_IN), lambda b, j: (b, jnp.minimum((j + 1) * sub, n_sub - 1), 0)),
                  pl.BlockSpec((TOK_TILE, LANES), lambda b, j: (j, 0)),
                  pl.BlockSpec((TOK_TILE, LANES), lambda b, j: (j, 0)),
                  full((1, D_RW_IN)), full((1, D_RW_IN)),
                  full((1, 2 * D_RW)), full((2 * LORA_W, 2 * D_RW)),
                  full((1, 2 * D_RW)), full((2 * LORA_A, 2 * D_RW)),
                  full((LORA_G, D_RW)),
                  full((1, D_RW)), full((1, D_RW)), full((1, D_RW)),
                  full((D_RW, D_RW))],
        out_specs=(tok(D_RW), tok(2 * D_RW), tok(2 * D_RW), tok(2 * D_RW), tok(2 * D_RW),
                   pl.BlockSpec((1, n_seg, 2 * D_RW), lambda b, j: (b, j, 0)), tok(D_RW), tok(D_RW)),
        scratch_shapes=[pltpu.VMEM((TOK_TILE, LANES), jnp.float32)],
        compiler_params=pltpu.CompilerParams(vmem_limit_bytes=VMEM_LIMIT),
        name="prep",
    )(p_rw, p_rw, p_rw, cos_t, sin_t, mu_prev, mu_next, w0, wup2, a0, aup2, gup, k_k, k_a, r_k, bd)


def _rope_tables():
    nf = HEAD_DIM // 4
    pos = np.arange(SEQ)
    inv = ROPE_THETA ** (-np.arange(nf, dtype=np.float32) / nf)
    lane = np.arange(LANES) % HEAD_DIM
    half, pair, f = lane // 32, (lane % 32) // 16, lane % 16
    coord = np.where(half[None, :] == 0, (pos // GRID_W)[:, None], (pos % GRID_W)[:, None]).astype(np.float32)
    ang = coord * inv[f][None, :].astype(np.float32)
    cos = np.cos(ang).astype(np.float32)
    sin = np.sin(ang).astype(np.float32) * np.where(pair == 0, -1.0, 1.0)[None, :].astype(np.float32)
    cos = np.concatenate([np.ones((CTX_LEN, LANES), np.float32), cos], axis=0)
    sin = np.concatenate([np.zeros((CTX_LEN, LANES), np.float32), sin], axis=0)
    return jnp.asarray(cos), jnp.asarray(sin)


def _scan_kernel(v_ref, abar_ref, rbar_ref, ktil_ref, btil_ref, gend_ref, o_ref, s_ref):
    d = pl.program_id(0)
    i = pl.program_id(1)

    @pl.when(i == 0)
    def _():
        s_ref[...] = jnp.zeros_like(s_ref)

    k_unroll = 8
    n_kb = HEAD_DIM // k_unroll
    zero = jnp.zeros((HEAD_DIM, LANES), jnp.float32)

    def make_step(with_out):
        def step(s, carry):
            t = jnp.where(d == 0, s, SCAN_CHUNK - 1 - s)
            vv = v_ref[t]

            def sk_body(kb, sk):
                for u in range(k_unroll):
                    k = kb * k_unroll + u
                    sk = sk + s_ref[k] * abar_ref[0, t, pl.ds(k, 1), :]
                return sk

            sk = lax.fori_loop(0, n_kb, sk_body, zero)

            def upd_body(kb, o):
                for u in range(k_unroll):
                    k = kb * k_unroll + u
                    s_new = s_ref[k] + (vv * ktil_ref[0, t, pl.ds(k, 1), :] - sk * btil_ref[0, t, pl.ds(k, 1), :])
                    s_ref[k] = s_new
                    if with_out:
                        o = o + s_new * rbar_ref[0, t, pl.ds(k, 1), :]
                return o

            o = lax.fori_loop(0, n_kb, upd_body, zero)
            if with_out:
                o_ref[0, t] = o
            return carry
        return step

    @pl.when(i < CTX_LEN // SCAN_CHUNK)
    def _():
        lax.fori_loop(0, SCAN_CHUNK, make_step(False), 0)

    @pl.when(i >= CTX_LEN // SCAN_CHUNK)
    def _():
        lax.fori_loop(0, SCAN_CHUNK, make_step(True), 0)

    def renorm(kb, carry):
        for u in range(k_unroll):
            k = kb * k_unroll + u
            s_ref[k] = s_ref[k] * gend_ref[0, 0, pl.ds(k, 1), :]
        return carry

    lax.fori_loop(0, n_kb, renorm, 0)


def _scan_call(v_t, abar_t, rbar_t, ktil_t, btil_t, gend_t):
    nc = CTX_LEN // SCAN_CHUNK
    nl = SEQ // SCAN_CHUNK

    def blk(d, i):
        rev = jnp.where(i < nc, nc - 1 - i, 2 * nc + nl - 1 - i)
        return jnp.where(d == 0, i, rev)

    def oblk(d, i):
        fwd = jnp.maximum(i - nc, 0)
        rev = jnp.where(i < nc, nl - 1, nc + nl - 1 - i)
        return jnp.where(d == 0, fwd, rev)

    shared = pl.BlockSpec((SCAN_CHUNK, HEAD_DIM, LANES), lambda d, i: (blk(d, i), 0, 0))
    perdir = pl.BlockSpec((1, SCAN_CHUNK, HEAD_DIM, LANES), lambda d, i: (d, blk(d, i), 0, 0))
    perchunk = pl.BlockSpec((1, 1, HEAD_DIM, LANES), lambda d, i: (d, blk(d, i), 0, 0))
    return pl.pallas_call(
        _scan_kernel,
        out_shape=jax.ShapeDtypeStruct((2, SEQ, HEAD_DIM, LANES), jnp.float32),
        grid=(2, nc + nl),
        in_specs=[shared, perdir, perdir, perdir, perdir, perchunk],
        out_specs=pl.BlockSpec((1, SCAN_CHUNK, HEAD_DIM, LANES), lambda d, i: (d, oblk(d, i), 0, 0)),
        scratch_shapes=[pltpu.VMEM((HEAD_DIM, HEAD_DIM, LANES), jnp.float32)],
        compiler_params=pltpu.CompilerParams(dimension_semantics=("arbitrary", "arbitrary"),
                                             vmem_limit_bytes=VMEM_LIMIT),
        name="scan",
    )(v_t, abar_t, rbar_t, ktil_t, btil_t, gend_t)


def _to_scan_layout(a):
    n = a.shape[-1] // D_RW
    t = a.shape[1]
    a = a.reshape(BATCH, t, n, H_RW, HEAD_DIM)
    return a.transpose(2, 1, 4, 0, 3).reshape(n, t, HEAD_DIM, LANES)


def _from_scan_layout(o):
    return o.reshape(SEQ, HEAD_DIM, BATCH, H_RW).transpose(2, 0, 3, 1).reshape(BATCH, SEQ, D_RW)


def _finish_kernel(na_ref, o_ref, bonus_ref, g_ref, x_ref, mod_ref, lng_ref, lnb_ref, wout_ref, n2g_ref, bd_ref,
                   wr_ref, br_ref, uj_ref, x1_out, h2_out, eid_out, wt_out, rank_out, cnt_out, run_ref):
    @pl.when((pl.program_id(0) == 0) & (pl.program_id(1) == 0))
    def _():
        run_ref[...] = jnp.zeros_like(run_ref)

    bd = bd_ref[...]
    o = o_ref[0]
    mu = _group_sum(o, bd) * (1.0 / HEAD_DIM)
    oc = o - mu
    var = _group_sum(oc * oc, bd) * (1.0 / HEAD_DIM)
    y = oc * lax.rsqrt(var + RW_LN_EPS) * lng_ref[...] + lnb_ref[...]
    rw = (y + bonus_ref[0]) * g_ref[0]
    mix = jnp.concatenate([na_ref[0], rw], axis=-1).astype(jnp.bfloat16)
    yx = jnp.dot(mix, wout_ref[...], preferred_element_type=jnp.float32)
    mod = mod_ref[0]
    x1 = x_ref[0] + mod[2:3, :] * yx
    x1_out[0] = x1
    h2 = _rmsnorm_rows(x1, n2g_ref[...]) * (1.0 + mod[4:5, :]) + mod[3:4, :]
    h2_out[0] = h2.astype(jnp.bfloat16)

    lg = lax.dot_general(wr_ref[...], h2, (((1,), (1,)), ((), ())), precision=_HI,
                         preferred_element_type=jnp.float32) + br_ref[...]
    gl = lg[0:MOE_GROUPS, :]
    grow = lax.broadcasted_iota(jnp.int32, gl.shape, 0)
    gmax = jnp.max(gl, axis=0, keepdims=True)
    g_sel = jnp.min(jnp.where(gl == gmax, grow, MOE_GROUPS), axis=0, keepdims=True)
    p_group = 1.0 / jnp.sum(jnp.exp(gl - gmax), axis=0, keepdims=True)
    el = jnp.zeros((MOE_PER_GROUP, TOK_TILE), jnp.float32)
    for gi in range(MOE_GROUPS):
        el = jnp.where(g_sel == gi, lg[8 + gi * MOE_PER_GROUP:8 + (gi + 1) * MOE_PER_GROUP, :], el)
    erow = lax.broadcasted_iota(jnp.int32, el.shape, 0)
    m1 = jnp.max(el, axis=0, keepdims=True)
    i1 = jnp.min(jnp.where(el == m1, erow, MOE_PER_GROUP), axis=0, keepdims=True)
    el2 = jnp.where(erow == i1, -jnp.inf, el)
    m2 = jnp.max(el2, axis=0, keepdims=True)
    i2 = jnp.min(jnp.where(el2 == m2, erow, MOE_PER_GROUP), axis=0, keepdims=True)
    e21 = jnp.exp(m2 - m1)
    e0 = g_sel * MOE_PER_GROUP + i1
    e1 = g_sel * MOE_PER_GROUP + i2
    eid_out[0:1, :] = e0
    eid_out[1:2, :] = e1
    wt_out[0:1, :] = p_group / (1.0 + e21)
    wt_out[1:2, :] = p_group * e21 / (1.0 + e21)

    xrow = lax.broadcasted_iota(jnp.int32, (N_EXPERTS, TOK_TILE), 0)
    m0 = xrow == e0
    m1b = xrow == e1
    c0 = jnp.dot(jnp.where(m0, 1.0, 0.0).astype(jnp.bfloat16), uj_ref[...], preferred_element_type=jnp.float32)
    c1 = jnp.dot(jnp.where(m1b, 1.0, 0.0).astype(jnp.bfloat16), uj_ref[...], preferred_element_type=jnp.float32)
    run = run_ref[...]
    tot0 = c0[:, TOK_TILE:]
    rank_out[0:1, :] = jnp.sum(jnp.where(m0, run + c0[:, :TOK_TILE], 0.0), axis=0, keepdims=True).astype(jnp.int32)
    rank_out[1:2, :] = jnp.sum(jnp.where(m1b, run + tot0 + c1[:, :TOK_TILE], 0.0), axis=0,
                               keepdims=True).astype(jnp.int32)
    run = run + tot0 + c1[:, TOK_TILE:]
    run_ref[...] = run
    cnt_out[...] = run


def _finish_call(na_x, o_rw, bonus, g, x, mod3, ln_g, ln_b, w_out_bf16, norm2_g, bd, wr_t, br, uj):
    n_lat = SEQ // TOK_TILE
    full = lambda shape: pl.BlockSpec(shape, lambda b, j: (0,) * len(shape))
    lat = lambda width: pl.BlockSpec((1, TOK_TILE, width), lambda b, j: (b, j, 0))
    allt = lambda width: pl.BlockSpec((1, TOK_TILE, width), lambda b, j: (b, j + 1, 0))
    tokl = pl.BlockSpec((2, TOK_TILE), lambda b, j: (0, b * n_lat + j))
    return pl.pallas_call(
        _finish_kernel,
        out_shape=(jax.ShapeDtypeStruct((BATCH, SEQ, D_MODEL), jnp.float32),
                   jax.ShapeDtypeStruct((BATCH, SEQ, D_MODEL), jnp.bfloat16),
                   jax.ShapeDtypeStruct((2, N_TOK), jnp.int32),
                   jax.ShapeDtypeStruct((2, N_TOK), jnp.float32),
                   jax.ShapeDtypeStruct((2, N_TOK), jnp.int32),
                   jax.ShapeDtypeStruct((N_EXPERTS, TOK_TILE), jnp.float32)),
        grid=(BATCH, n_lat),
        in_specs=[lat(D_NA), lat(D_RW), allt(D_RW), allt(D_RW), lat(D_MODEL),
                  pl.BlockSpec((1, 6, D_MODEL), lambda b, j: (b, 0, 0)),
                  full((1, D_RW)), full((1, D_RW)), full((D_MODEL, D_MODEL)), full((1, D_MODEL)),
                  full((D_RW, D_RW)), full((LOGIT_ROWS, D_MODEL)), full((LOGIT_ROWS, 1)),
                  full((TOK_TILE, 2 * TOK_TILE))],
        out_specs=(lat(D_MODEL), lat(D_MODEL), tokl, tokl, tokl, full((N_EXPERTS, TOK_TILE))),
        scratch_shapes=[pltpu.VMEM((N_EXPERTS, TOK_TILE), jnp.float32)],
        compiler_params=pltpu.CompilerParams(dimension_semantics=("arbitrary", "arbitrary"),
                                             vmem_limit_bytes=VMEM_LIMIT),
        name="finish",
    )(na_x, o_rw, bonus, g, x, mod3, ln_g, ln_b, w_out_bf16, norm2_g, bd, wr_t, br, uj)


def _moe_kernel(te_ref, tv_ref, tx_ref, xs_ref, w1_ref, w3_ref, w2_ref, y_ref, w1s, w3s, w2s):
    t = pl.program_id(0)

    @pl.when((t == 0) | (te_ref[t] != te_ref[jnp.maximum(t - 1, 0)]))
    def _():
        w1s[...] = w1_ref[0].astype(jnp.bfloat16)
        w3s[...] = w3_ref[0].astype(jnp.bfloat16)
        w2s[...] = w2_ref[0].astype(jnp.bfloat16)

    @pl.when(tv_ref[t] == 1)
    def _():
        xs = xs_ref[...]
        h1 = jnp.dot(xs, w1s[...], preferred_element_type=jnp.float32)
        h3 = jnp.dot(xs, w3s[...], preferred_element_type=jnp.float32)
        he = (_silu(h1) * h3).astype(jnp.bfloat16)
        y_ref[...] = jnp.dot(he, w2s[...], preferred_element_type=jnp.float32).astype(jnp.bfloat16)

    @pl.when(tv_ref[t] == 0)
    def _():
        y_ref[...] = jnp.zeros_like(y_ref)


def _moe_call(tile_e, tile_valid, tile_src, xs, w1, w3, w2):
    return pl.pallas_call(
        _moe_kernel,
        out_shape=jax.ShapeDtypeStruct((N_SLOTS, D_MODEL), jnp.bfloat16),
        grid_spec=pltpu.PrefetchScalarGridSpec(
            num_scalar_prefetch=3,
            grid=(N_MOE_TILES,),
            in_specs=[pl.BlockSpec((MOE_TILE, D_MODEL), lambda t, te, tv, tx: (tx[t], 0)),
                      pl.BlockSpec((1, D_MODEL, D_EXPERT), lambda t, te, tv, tx: (te[t], 0, 0)),
                      pl.BlockSpec((1, D_MODEL, D_EXPERT), lambda t, te, tv, tx: (te[t], 0, 0)),
                      pl.BlockSpec((1, D_EXPERT, D_MODEL), lambda t, te, tv, tx: (te[t], 0, 0))],
            out_specs=pl.BlockSpec((MOE_TILE, D_MODEL), lambda t, te, tv, tx: (t, 0)),
            scratch_shapes=[pltpu.VMEM((D_MODEL, D_EXPERT), jnp.bfloat16),
                            pltpu.VMEM((D_MODEL, D_EXPERT), jnp.bfloat16),
                            pltpu.VMEM((D_EXPERT, D_MODEL), jnp.bfloat16)]),
        compiler_params=pltpu.CompilerParams(dimension_semantics=("arbitrary",), vmem_limit_bytes=VMEM_LIMIT),
        name="moe",
    )(tile_e, tile_valid, tile_src, xs, w1, w3, w2)


def _moe_plan(eid, rank, cnt):
    counts = cnt[:, 0].astype(jnp.int32)
    gsz = ((counts + MOE_TILE - 1) // MOE_TILE) * MOE_TILE
    gend = jnp.cumsum(gsz)
    goff = gend - gsz
    pos = rank
    for e in range(N_EXPERTS):
        pos = pos + jnp.where(eid == e, goff[e], 0)
    starts = jnp.arange(N_MOE_TILES, dtype=jnp.int32) * MOE_TILE
    te = jnp.sum((gend[None, :] <= starts[:, None]).astype(jnp.int32), axis=1)
    valid = (te < N_EXPERTS).astype(jnp.int32)
    last = jnp.max(jnp.where(counts > 0, jnp.arange(N_EXPERTS), 0)).astype(jnp.int32)
    te = jnp.where(valid == 1, te, last)
    n_valid = gend[-1] // MOE_TILE
    tx = jnp.minimum(jnp.arange(N_MOE_TILES, dtype=jnp.int32), n_valid - 1)
    tok = jnp.broadcast_to(jnp.arange(N_TOK, dtype=jnp.int32), (2, N_TOK))
    src_tok = jnp.zeros((N_SLOTS,), jnp.int32).at[pos.reshape(-1)].set(tok.reshape(-1))
    return pos, src_tok, te, valid, tx


def _final_kernel(x1_ref, ya_ref, yb_ref, wt_ref, mod_ref, o_ref):
    w = wt_ref[...]
    ya = ya_ref[0].astype(jnp.float32)
    yb = yb_ref[0].astype(jnp.float32)
    o_ref[0] = x1_ref[0] + mod_ref[0][5:6, :] * (w[:, 0:1] * ya + w[:, 1:2] * yb)


def _final_call(x1, ya, yb, wt_rows, mod3):
    n_lat = SEQ // TOK_TILE
    lat = pl.BlockSpec((1, TOK_TILE, D_MODEL), lambda b, j: (b, j, 0))
    return pl.pallas_call(
        _final_kernel,
        out_shape=jax.ShapeDtypeStruct((BATCH, SEQ, D_MODEL), jnp.float32),
        grid=(BATCH, n_lat),
        in_specs=[lat, lat, lat, pl.BlockSpec((TOK_TILE, 2), lambda b, j: (b * n_lat + j, 0)),
                  pl.BlockSpec((1, 6, D_MODEL), lambda b, j: (b, 0, 0))],
        out_specs=lat,
        name="final",
    )(x1, ya, yb, wt_rows, mod3)


def _block_diag2(w):
    z = jnp.zeros_like(w[0])
    return jnp.concatenate([jnp.concatenate([w[0], z], axis=1), jnp.concatenate([z, w[1]], axis=1)], axis=0)


def kernel(x, c, ctx, c_ctx, w_mod, b_mod, norm1_g, norm2_g, w_in, na_q_g, na_k_g, na_rpb, rw_mu_prev, rw_mu_next,
           rw_w0, rw_w_up, rw_a0, rw_a_up, rw_g_up, rw_k_k, rw_k_a, rw_r_k, rw_ln_g, rw_ln_b, w_out, moe_wg, moe_bg,
           moe_we, moe_be, moe_w1, moe_w3, moe_w2):
    bf = jnp.bfloat16
    mod_rows = BATCH + 8
    cs = jnp.concatenate([c, c_ctx[None, :], jnp.zeros((mod_rows - BATCH - 1, D_MODEL), jnp.float32)], axis=0)
    mod = _mod_call(cs, w_mod[0], b_mod[0][None, :])
    mod3 = mod.reshape(mod_rows, 6, D_MODEL)

    head = jnp.arange(D_RW) // HEAD_DIM
    bd = (head[:, None] == head[None, :]).astype(bf)
    qkv, p_rw = _inproj_call(x, ctx, mod3, norm1_g[0][None, :], w_in[0].astype(bf),
                             jnp.tile(na_q_g[0], H_NA)[None, :], jnp.tile(na_k_g[0], H_NA)[None, :], bd)

    na_x = _na_call(qkv, _na_bias_table(na_rpb[0]))

    cos_t, sin_t = _rope_tables()
    v, abar, rbar, ktil, btil, gend, g, bonus = _prep_call(
        p_rw, cos_t, sin_t, rw_mu_prev[0][None, :], rw_mu_next[0][None, :],
        rw_w0[0].reshape(1, 2 * D_RW), _block_diag2(rw_w_up[0]).astype(bf),
        rw_a0[0].reshape(1, 2 * D_RW), _block_diag2(rw_a_up[0]).astype(bf),
        rw_g_up[0].astype(bf), rw_k_k[0][None, :], rw_k_a[0][None, :], rw_r_k[0].reshape(1, D_RW), bd)

    o2 = _scan_call(_to_scan_layout(v)[0], _to_scan_layout(abar), _to_scan_layout(rbar), _to_scan_layout(ktil),
                    _to_scan_layout(btil), _to_scan_layout(gend))
    o_rw = _from_scan_layout(o2[0] + o2[1])

    wr_t = jnp.zeros((LOGIT_ROWS, D_MODEL), jnp.float32)
    wr_t = wr_t.at[0:MOE_GROUPS].set(moe_wg[0].T)
    wr_t = wr_t.at[8:].set(moe_we[0].transpose(0, 2, 1).reshape(N_EXPERTS, D_MODEL))
    br = jnp.zeros((LOGIT_ROWS,), jnp.float32).at[0:MOE_GROUPS].set(moe_bg[0]).at[8:].set(moe_be[0].reshape(-1))
    tri = jnp.arange(TOK_TILE)
    uj = jnp.concatenate([(tri[:, None] < tri[None, :]).astype(bf), jnp.ones((TOK_TILE, TOK_TILE), bf)], axis=1)
    x1, h2p, eid, wts, rank, cnt = _finish_call(na_x, o_rw, bonus, g, x, mod3, rw_ln_g[0][None, :],
                                                rw_ln_b[0][None, :], w_out[0].astype(bf), norm2_g[0][None, :], bd,
                                                wr_t, br[:, None], uj)

    pos, src_tok, tile_e, tile_valid, tile_src = _moe_plan(eid, rank, cnt)
    xs = jnp.take(h2p.reshape(N_TOK, D_MODEL), src_tok, axis=0)
    ys = _moe_call(tile_e, tile_valid, tile_src, xs, moe_w1[0], moe_w3[0], moe_w2[0])
    ya = jnp.take(ys, pos[0], axis=0).reshape(BATCH, SEQ, D_MODEL)
    yb = jnp.take(ys, pos[1], axis=0).reshape(BATCH, SEQ, D_MODEL)
    return _final_call(x1, ya, yb, wts.T, mod3)
```

```python
import functools
import math

import jax
import jax.numpy as jnp
import numpy as np
from jax import lax
from jax.experimental import pallas as pl
from jax.experimental.pallas import tpu as pltpu

D_MODEL = 1024
BATCH = 16
SEQ = 2048
GRID_W = 64
GRID_H = SEQ // GRID_W
CTX_LEN = 256
T_ALL = CTX_LEN + SEQ
HEAD_DIM = 64
D_NA = 512
D_RW = 512
H_NA = D_NA // HEAD_DIM
H_RW = D_RW // HEAD_DIM
NA_KH = 8
NA_KW = 16
LORA_W = 64
LORA_A = 64
LORA_G = 128
D_RW_IN = 3 * D_RW + 2 * (LORA_W + LORA_A) + LORA_G
D_IN = 3 * D_NA + D_RW_IN
MOE_GROUPS = 4
MOE_PER_GROUP = 8
N_EXPERTS = MOE_GROUPS * MOE_PER_GROUP
D_EXPERT = 512
ROPE_THETA = 10000.0
NORM_EPS = 1e-6
RW_LN_EPS = 64e-5
NEG = -1e30
DECAY_SCALE = math.exp(-0.5)

LANES = 128
TOK_TILE = 256
N_TILES = T_ALL // TOK_TILE
SCAN_CHUNK = 16
MOE_TILE = 256
N_TOK = BATCH * SEQ
N_SLOTS = 2 * N_TOK + N_EXPERTS * MOE_TILE
N_MOE_TILES = N_SLOTS // MOE_TILE
LOGIT_ROWS = 8 + N_EXPERTS
VMEM_LIMIT = 48 * 1024 * 1024

_HI = lax.Precision.HIGHEST


def _sigmoid(x):
    return 1.0 / (1.0 + jnp.exp(-x))


def _silu(x):
    return x * _sigmoid(x)


def _rmsnorm_rows(xf, g):
    return xf * lax.rsqrt(jnp.mean(xf * xf, axis=-1, keepdims=True) + NORM_EPS) * g


def _group_sum(x, bd):
    hi = x.astype(jnp.bfloat16)
    lo = (x - hi.astype(jnp.float32)).astype(jnp.bfloat16)
    return (jnp.dot(hi, bd, preferred_element_type=jnp.float32)
            + jnp.dot(lo, bd, preferred_element_type=jnp.float32))


def _mod_kernel(c_ref, w_ref, b_ref, o_ref):
    o_ref[...] = jnp.dot(_silu(c_ref[...]), w_ref[...], precision=_HI,
                         preferred_element_type=jnp.float32) + b_ref[...]


def _mod_call(cs, w_mod, b_mod):
    rows = cs.shape[0]
    n = w_mod.shape[1]
    blk = 1024
    return pl.pallas_call(
        _mod_kernel,
        out_shape=jax.ShapeDtypeStruct((rows, n), jnp.float32),
        grid=(n // blk,),
        in_specs=[pl.BlockSpec((rows, D_MODEL), lambda j: (0, 0)),
                  pl.BlockSpec((D_MODEL, blk), lambda j: (0, j)),
                  pl.BlockSpec((1, blk), lambda j: (0, j))],
        out_specs=pl.BlockSpec((rows, blk), lambda j: (0, j)),
        name="mod",
    )(cs, w_mod, b_mod)


def _inproj_kernel(x_ref, ctx_ref, modx_ref, modc_ref, g_ref, w_ref, qg_ref, kg_ref, bd_ref, na_ref, rw_ref):
    is_ctx = pl.program_id(1) == 0
    xin = jnp.where(is_ctx, ctx_ref[0], x_ref[0])
    mod = jnp.where(is_ctx, modc_ref[0], modx_ref[0])
    h = _rmsnorm_rows(xin, g_ref[...]) * (1.0 + mod[1:2, :]) + mod[0:1, :]
    p = jnp.dot(h.astype(jnp.bfloat16), w_ref[...], preferred_element_type=jnp.float32)
    rw_ref[0] = p[:, 3 * D_NA:]
    bd = bd_ref[...]
    q = p[:, 0:D_NA]
    k = p[:, D_NA:2 * D_NA]
    qn = q * lax.rsqrt(_group_sum(q * q, bd) * (1.0 / HEAD_DIM) + NORM_EPS) * (qg_ref[...] * HEAD_DIM ** -0.5)
    kn = k * lax.rsqrt(_group_sum(k * k, bd) * (1.0 / HEAD_DIM) + NORM_EPS) * kg_ref[...]
    na_ref[0, :, 0:D_NA] = qn.astype(jnp.bfloat16)
    na_ref[0, :, D_NA:2 * D_NA] = kn.astype(jnp.bfloat16)
    na_ref[0, :, 2 * D_NA:] = p[:, 2 * D_NA:3 * D_NA].astype(jnp.bfloat16)


def _inproj_call(x, ctx, mod3, norm_g, w_in_bf16, q_g, k_g, bd):
    return pl.pallas_call(
        _inproj_kernel,
        out_shape=(jax.ShapeDtypeStruct((BATCH, T_ALL, 3 * D_NA), jnp.bfloat16),
                   jax.ShapeDtypeStruct((BATCH, T_ALL, D_RW_IN), jnp.float32)),
        grid=(BATCH, N_TILES),
        in_specs=[pl.BlockSpec((1, TOK_TILE, D_MODEL), lambda b, j: (b, jnp.maximum(j - 1, 0), 0)),
                  pl.BlockSpec((1, TOK_TILE, D_MODEL), lambda b, j: (b, 0, 0)),
                  pl.BlockSpec((1, 6, D_MODEL), lambda b, j: (b, 0, 0)),
                  pl.BlockSpec((1, 6, D_MODEL), lambda b, j: (BATCH, 0, 0)),
                  pl.BlockSpec((1, D_MODEL), lambda b, j: (0, 0)),
                  pl.BlockSpec((D_MODEL, D_IN), lambda b, j: (0, 0)),
                  pl.BlockSpec((1, D_NA), lambda b, j: (0, 0)),
                  pl.BlockSpec((1, D_NA), lambda b, j: (0, 0)),
                  pl.BlockSpec((D_NA, D_NA), lambda b, j: (0, 0))],
        out_specs=(pl.BlockSpec((1, TOK_TILE, 3 * D_NA), lambda b, j: (b, j, 0)),
                   pl.BlockSpec((1, TOK_TILE, D_RW_IN), lambda b, j: (b, j, 0))),
        compiler_params=pltpu.CompilerParams(vmem_limit_bytes=VMEM_LIMIT),
        name="inproj",
    )(x, ctx, mod3, mod3, norm_g, w_in_bf16, q_g, k_g, bd)


NA_QROWS = 4
NA_QBLK = NA_QROWS * GRID_W
NA_BAND = NA_KH + NA_QROWS - 1
NA_BAND_KEYS = NA_BAND * GRID_W
NA_NBLK = GRID_H // NA_QROWS


def _na_band_start(i):
    return np.clip(i * NA_QROWS - NA_KH // 2, 0, GRID_H - NA_BAND)


def _na_kernel(q_ref, k_ref, v_ref, tz_ref, o_ref, qs, ks, vs, bias_ref):
    @pl.when(pl.program_id(1) == 0)
    def _():
        masked = jnp.full((GRID_W, GRID_W), NEG, jnp.float32)
        for h in range(2):
            for p, i in enumerate((0, 1, NA_NBLK - 1)):
                bs = int(_na_band_start(i))
                for g in range(NA_QROWS):
                    qr = i * NA_QROWS + g
                    rs = int(np.clip(qr - NA_KH // 2, 0, GRID_H - NA_KH))
                    for j in range(NA_BAND):
                        kr = bs + j
                        blk = tz_ref[h, kr - qr + NA_KH - 1] if rs <= kr < rs + NA_KH else masked
                        bias_ref[h, p, g * GRID_W:(g + 1) * GRID_W, j * GRID_W:(j + 1) * GRID_W] = blk

    for h in range(2):
        hs = slice(h * HEAD_DIM, (h + 1) * HEAD_DIM)
        qs[h] = q_ref[0, CTX_LEN:, hs]
        ks[h] = k_ref[0, :, hs]
        vs[h] = v_ref[0, :, hs]

    nt = (((1,), (1,)), ((), ()))

    def body(i, carry):
        bs = jnp.clip(i * NA_QROWS - NA_KH // 2, 0, GRID_H - NA_BAND)
        pattern = jnp.where(i == 0, 0, jnp.where(i == NA_NBLK - 1, 2, 1))
        q0 = pl.multiple_of(i * NA_QBLK, NA_QBLK)
        k0 = pl.multiple_of(CTX_LEN + bs * GRID_W, GRID_W)
        for h in range(2):
            qb = qs[h, pl.ds(q0, NA_QBLK), :]
            kw = ks[h, pl.ds(k0, NA_BAND_KEYS), :]
            vw = vs[h, pl.ds(k0, NA_BAND_KEYS), :]
            kc = ks[h, 0:CTX_LEN, :]
            vc = vs[h, 0:CTX_LEN, :]
            s_win = lax.dot_general(qb, kw, nt, preferred_element_type=jnp.float32) + bias_ref[h, pattern]
            s_ctx = lax.dot_general(qb, kc, nt, preferred_element_type=jnp.float32)
            m = jnp.maximum(jnp.max(s_win, axis=-1, keepdims=True), jnp.max(s_ctx, axis=-1, keepdims=True))
            e_win = jnp.exp(s_win - m)
            e_ctx = jnp.exp(s_ctx - m)
            den = jnp.sum(e_win, axis=-1, keepdims=True) + jnp.sum(e_ctx, axis=-1, keepdims=True)
            o = (jnp.dot(e_win.astype(jnp.bfloat16), vw, preferred_element_type=jnp.float32)
                 + jnp.dot(e_ctx.astype(jnp.bfloat16), vc, preferred_element_type=jnp.float32))
            o_ref[0, pl.ds(q0, NA_QBLK), h * HEAD_DIM:(h + 1) * HEAD_DIM] = o / den
        return carry

    lax.fori_loop(0, NA_NBLK, body, 0)


def _na_call(qkv, bias8):
    n_hp = D_NA // LANES
    return pl.pallas_call(
        _na_kernel,
        out_shape=jax.ShapeDtypeStruct((BATCH, SEQ, D_NA), jnp.float32),
        grid=(n_hp, BATCH),
        in_specs=[pl.BlockSpec((1, T_ALL, LANES), lambda hp, b: (b, 0, hp)),
                  pl.BlockSpec((1, T_ALL, LANES), lambda hp, b: (b, 0, n_hp + hp)),
                  pl.BlockSpec((1, T_ALL, LANES), lambda hp, b: (b, 0, 2 * n_hp + hp)),
                  pl.BlockSpec((2, 2 * NA_KH - 1, GRID_W, GRID_W), lambda hp, b: (hp, 0, 0, 0))],
        out_specs=pl.BlockSpec((1, SEQ, LANES), lambda hp, b: (b, 0, hp)),
        scratch_shapes=[pltpu.VMEM((2, SEQ, HEAD_DIM), jnp.bfloat16),
                        pltpu.VMEM((2, T_ALL, HEAD_DIM), jnp.bfloat16),
                        pltpu.VMEM((2, T_ALL, HEAD_DIM), jnp.bfloat16),
                        pltpu.VMEM((2, 3, NA_QBLK, NA_BAND_KEYS), jnp.float32)],
        compiler_params=pltpu.CompilerParams(dimension_semantics=("arbitrary", "arbitrary"),
                                             vmem_limit_bytes=VMEM_LIMIT),
        name="na",
    )(qkv, qkv, qkv, bias8)


def _na_bias_table(rpb):
    qc = np.arange(GRID_W)
    cs = np.clip(qc - NA_KW // 2, 0, GRID_W - NA_KW)
    kc = np.arange(GRID_W)
    col_ok = (kc[None, :] >= cs[:, None]) & (kc[None, :] < cs[:, None] + NA_KW)
    dc = np.clip(kc[None, :] - qc[:, None], -(NA_KW - 1), NA_KW - 1) + NA_KW - 1
    return jnp.where(col_ok[None, None], rpb[:, :, dc], NEG).astype(jnp.float32)


def _swap16(x):
    lane = lax.broadcasted_iota(jnp.int32, x.shape, 1)
    return jnp.where((lane & 16) == 0, pltpu.roll(x, LANES - 16, axis=1), pltpu.roll(x, 16, axis=1))


def _rope(x, cos, sin):
    blocks = []
    for i in range(x.shape[1] // LANES):
        xb = x[:, i * LANES:(i + 1) * LANES]
        blocks.append(xb * cos + _swap16(xb) * sin)
    return jnp.concatenate(blocks, axis=1)


def _shift_rows(x, n, fill, down):
    if down:
        return jnp.concatenate([fill, x[:-n]], axis=0)
    return jnp.concatenate([x[n:], fill], axis=0)


def _chunk_cumprod(w, reverse):
    inc = w
    s = 1
    while s < SCAN_CHUNK:
        ones = jnp.ones((s * BATCH, w.shape[1]), jnp.float32)
        inc = inc * _shift_rows(inc, s * BATCH, ones, down=not reverse)
        s *= 2
    exc = _shift_rows(inc, BATCH, jnp.ones((BATCH, w.shape[1]), jnp.float32), down=not reverse)
    return inc, exc


def _scan_tiles(u):
    low = lax.broadcasted_iota(jnp.int32, (BATCH, LANES), 1) < HEAD_DIM
    steps = u.shape[0] // BATCH
    tiles = []
    for m in range(0, steps, 2):
        a = u[m * BATCH:(m + 1) * BATCH]
        b = u[(m + 1) * BATCH:(m + 2) * BATCH] if m + 1 < steps else a
        rows = []
        for hp in range(D_RW // LANES):
            pa = a[:, hp * LANES:(hp + 1) * LANES]
            pb = b[:, hp * LANES:(hp + 1) * LANES]
            rows.append(jnp.where(low, pa, pltpu.roll(pb, HEAD_DIM, axis=1)))
            rows.append(jnp.where(low, pltpu.roll(pa, HEAD_DIM, axis=1), pb))
        wt = jnp.concatenate(rows, axis=0).T
        tiles.append(wt[:HEAD_DIM])
        if m + 1 < steps:
            tiles.append(wt[HEAD_DIM:])
    return tiles


def _prep_kernel(p_ref, pv_ref, nx_ref, cos_ref, sin_ref, mup_ref, mun_ref, w0_ref, wup_ref, a0_ref, aup_ref,
                 gup_ref, kk_ref, ka_ref, rk_ref, bd_ref,
                 v_out, abar_out, rbar_out, ktil_out, btil_out, gend_out, g_out, bonus_out):
    j = pl.program_id(0)
    n_ctx = CTX_LEN // SCAN_CHUNK
    p = p_ref[...]
    seq_start = (j == 0) | (j == n_ctx)
    seq_end = (j == n_ctx - 1) | (j == pl.num_programs(0) - 1)
    prev = _shift_rows(p, BATCH, jnp.where(seq_start, 0.0, pv_ref[...]), down=True)
    nxt = _shift_rows(p, BATCH, jnp.where(seq_end, 0.0, nx_ref[...]), down=False)
    ps = p + mup_ref[...] * (prev - p) + mun_ref[...] * (nxt - p)

    cos = cos_ref[...]
    sin = sin_ref[...]
    r = _rope(ps[:, 0:D_RW], cos, sin)
    k = _rope(ps[:, D_RW:2 * D_RW], cos, sin)
    v = ps[:, 2 * D_RW:3 * D_RW]
    o3 = 3 * D_RW
    wd = ps[:, o3:o3 + 2 * LORA_W]
    ad = ps[:, o3 + 2 * LORA_W:o3 + 2 * LORA_W + 2 * LORA_A]
    gd = ps[:, o3 + 2 * LORA_W + 2 * LORA_A:]

    w_pre = w0_ref[...] + jnp.dot(jnp.tanh(wd).astype(jnp.bfloat16), wup_ref[...], preferred_element_type=jnp.float32)
    decay = jnp.exp(-DECAY_SCALE * _sigmoid(w_pre))
    a = _sigmoid(a0_ref[...] + jnp.dot(ad.astype(jnp.bfloat16), aup_ref[...], preferred_element_type=jnp.float32))
    g_out[...] = jnp.dot(_sigmoid(gd).astype(jnp.bfloat16), gup_ref[...], preferred_element_type=jnp.float32)

    bd = bd_ref[...]
    kk = k * kk_ref[...]
    kk = kk / jnp.maximum(jnp.sqrt(_group_sum(kk * kk, bd)), 1e-12)
    bonus_out[...] = _group_sum(r * k * rk_ref[...], bd) * v
    ka = ka_ref[...]

    for t, tile in enumerate(_scan_tiles(v)):
        v_out[t] = tile
    for d in range(2):
        ds_ = slice(d * D_RW, (d + 1) * D_RW)
        a_d = a[:, ds_]
        gam, gam_excl = _chunk_cumprod(decay[:, ds_], reverse=(d == 1))
        inv = 1.0 / gam
        streams = ((abar_out, gam_excl * kk), (rbar_out, gam * r),
                   (ktil_out, k * (1.0 + (a_d - 1.0) * ka) * inv), (btil_out, kk * a_d * inv))
        for out, val in streams:
            for t, tile in enumerate(_scan_tiles(val)):
                out[d, t] = tile
        gam_last = gam[:BATCH] if d == 1 else gam[(SCAN_CHUNK - 1) * BATCH:]
        gend_out[d, 0] = _scan_tiles(gam_last)[0]


def _prep_call(p_rw_t, cos_t, sin_t, mu_prev, mu_next, w0, wup2, a0, aup2, gup, k_k, k_a, r_k, bd):
    rows = SCAN_CHUNK * BATCH
    n_chunks = T_ALL // SCAN_CHUNK
    full = lambda shape: pl.BlockSpec(shape, lambda j: (0,) * len(shape))
    tok = lambda width: pl.BlockSpec((rows, width), lambda j: (j, 0))
    perdir = pl.BlockSpec((2, SCAN_CHUNK, HEAD_DIM, LANES), lambda j: (0, j, 0, 0))
    two = jax.ShapeDtypeStruct((2, T_ALL, HEAD_DIM, LANES), jnp.float32)
    nat = jax.ShapeDtypeStruct((T_ALL * BATCH, D_RW), jnp.float32)
    return pl.pallas_call(
        _prep_kernel,
        out_shape=(jax.ShapeDtypeStruct((T_ALL, HEAD_DIM, LANES), jnp.float32), two, two, two, two,
                   jax.ShapeDtypeStruct((2, n_chunks, HEAD_DIM, LANES), jnp.float32), nat, nat),
        grid=(n_chunks,),
        in_specs=[tok(D_RW_IN),
                  pl.BlockSpec((BATCH, D_RW_IN), lambda j: (jnp.maximum(j * SCAN_CHUNK - 1, 0), 0)),
                  pl.BlockSpec((BATCH, D_RW_IN), lambda j: (jnp.minimum((j + 1) * SCAN_CHUNK, T_ALL - 1), 0)),
                  tok(LANES), tok(LANES),
                  full((1, D_RW_IN)), full((1, D_RW_IN)),
                  full((1, 2 * D_RW)), full((2 * LORA_W, 2 * D_RW)),
                  full((1, 2 * D_RW)), full((2 * LORA_A, 2 * D_RW)),
                  full((LORA_G, D_RW)),
                  full((1, D_RW)), full((1, D_RW)), full((1, D_RW)),
                  full((D_RW, D_RW))],
        out_specs=(pl.BlockSpec((SCAN_CHUNK, HEAD_DIM, LANES), lambda j: (j, 0, 0)), perdir, perdir, perdir, perdir,
                   pl.BlockSpec((2, 1, HEAD_DIM, LANES), lambda j: (0, j, 0, 0)), tok(D_RW), tok(D_RW)),
        compiler_params=pltpu.CompilerParams(vmem_limit_bytes=VMEM_LIMIT),
        name="prep",
    )(p_rw_t, p_rw_t, p_rw_t, cos_t, sin_t, mu_prev, mu_next, w0, wup2, a0, aup2, gup, k_k, k_a, r_k, bd)


def _rope_tables():
    nf = HEAD_DIM // 4
    pos = np.arange(SEQ)
    inv = ROPE_THETA ** (-np.arange(nf, dtype=np.float32) / nf)
    lane = np.arange(LANES) % HEAD_DIM
    half, pair, f = lane // 32, (lane % 32) // 16, lane % 16
    coord = np.where(half[None, :] == 0, (pos // GRID_W)[:, None], (pos % GRID_W)[:, None]).astype(np.float32)
    ang = coord * inv[f][None, :].astype(np.float32)
    cos = np.cos(ang).astype(np.float32)
    sin = np.sin(ang).astype(np.float32) * np.where(pair == 0, -1.0, 1.0)[None, :].astype(np.float32)
    cos = np.concatenate([np.ones((CTX_LEN, LANES), np.float32), cos], axis=0)
    sin = np.concatenate([np.zeros((CTX_LEN, LANES), np.float32), sin], axis=0)
    return jnp.asarray(np.repeat(cos, BATCH, axis=0)), jnp.asarray(np.repeat(sin, BATCH, axis=0))


def _scan_kernel(v_ref, abar_ref, rbar_ref, ktil_ref, btil_ref, gend_ref, o_ref, s_ref):
    d = pl.program_id(0)
    i = pl.program_id(1)

    @pl.when(i == 0)
    def _():
        s_ref[...] = jnp.zeros_like(s_ref)

    k_unroll = 8
    n_kb = HEAD_DIM // k_unroll
    zero = jnp.zeros((HEAD_DIM, LANES), jnp.float32)

    def make_step(with_out):
        def step(s, carry):
            t = jnp.where(d == 0, s, SCAN_CHUNK - 1 - s)
            vv = v_ref[t]

            def sk_body(kb, sk):
                for u in range(k_unroll):
                    k = kb * k_unroll + u
                    sk = sk + s_ref[k] * abar_ref[0, t, pl.ds(k, 1), :]
                return sk

            sk = lax.fori_loop(0, n_kb, sk_body, zero)

            def upd_body(kb, o):
                for u in range(k_unroll):
                    k = kb * k_unroll + u
                    s_new = s_ref[k] + (vv * ktil_ref[0, t, pl.ds(k, 1), :] - sk * btil_ref[0, t, pl.ds(k, 1), :])
                    s_ref[k] = s_new
                    if with_out:
                        o = o + s_new * rbar_ref[0, t, pl.ds(k, 1), :]
                return o

            o = lax.fori_loop(0, n_kb, upd_body, zero)
            if with_out:
                o_ref[0, t] = o
            return carry
        return step

    @pl.when(i < CTX_LEN // SCAN_CHUNK)
    def _():
        lax.fori_loop(0, SCAN_CHUNK, make_step(False), 0)

    @pl.when(i >= CTX_LEN // SCAN_CHUNK)
    def _():
        lax.fori_loop(0, SCAN_CHUNK, make_step(True), 0)

    def renorm(kb, carry):
        for u in range(k_unroll):
            k = kb * k_unroll + u
            s_ref[k] = s_ref[k] * gend_ref[0, 0, pl.ds(k, 1), :]
        return carry

    lax.fori_loop(0, n_kb, renorm, 0)


def _scan_call(v_t, abar_t, rbar_t, ktil_t, btil_t, gend_t):
    nc = CTX_LEN // SCAN_CHUNK
    nl = SEQ // SCAN_CHUNK

    def blk(d, i):
        rev = jnp.where(i < nc, nc - 1 - i, 2 * nc + nl - 1 - i)
        return jnp.where(d == 0, i, rev)

    def oblk(d, i):
        fwd = jnp.maximum(i - nc, 0)
        rev = jnp.where(i < nc, nl - 1, nc + nl - 1 - i)
        return jnp.where(d == 0, fwd, rev)

    shared = pl.BlockSpec((SCAN_CHUNK, HEAD_DIM, LANES), lambda d, i: (blk(d, i), 0, 0))
    perdir = pl.BlockSpec((1, SCAN_CHUNK, HEAD_DIM, LANES), lambda d, i: (d, blk(d, i), 0, 0))
    perchunk = pl.BlockSpec((1, 1, HEAD_DIM, LANES), lambda d, i: (d, blk(d, i), 0, 0))
    return pl.pallas_call(
        _scan_kernel,
        out_shape=jax.ShapeDtypeStruct((2, SEQ, HEAD_DIM, LANES), jnp.float32),
        grid=(2, nc + nl),
        in_specs=[shared, perdir, perdir, perdir, perdir, perchunk],
        out_specs=pl.BlockSpec((1, SCAN_CHUNK, HEAD_DIM, LANES), lambda d, i: (d, oblk(d, i), 0, 0)),
        scratch_shapes=[pltpu.VMEM((HEAD_DIM, HEAD_DIM, LANES), jnp.float32)],
        compiler_params=pltpu.CompilerParams(dimension_semantics=("arbitrary", "arbitrary"),
                                             vmem_limit_bytes=VMEM_LIMIT),
        name="scan",
    )(v_t, abar_t, rbar_t, ktil_t, btil_t, gend_t)


def _from_scan_layout(o):
    return o.reshape(SEQ, HEAD_DIM, H_RW, BATCH).transpose(3, 0, 2, 1).reshape(BATCH, SEQ, D_RW)


def _finish_kernel(na_ref, o_ref, bonus_ref, g_ref, x_ref, mod_ref, lng_ref, lnb_ref, wout_ref, n2g_ref, bd_ref,
                   wr_ref, br_ref, uj_ref, x1_out, h2_out, eid_out, wt_out, rank_out, cnt_out, run_ref):
    @pl.when((pl.program_id(0) == 0) & (pl.program_id(1) == 0))
    def _():
        run_ref[...] = jnp.zeros_like(run_ref)

    bd = bd_ref[...]
    o = o_ref[0]
    mu = _group_sum(o, bd) * (1.0 / HEAD_DIM)
    oc = o - mu
    var = _group_sum(oc * oc, bd) * (1.0 / HEAD_DIM)
    y = oc * lax.rsqrt(var + RW_LN_EPS) * lng_ref[...] + lnb_ref[...]
    rw = (y + bonus_ref[0]) * g_ref[0]
    mix = jnp.concatenate([na_ref[0], rw], axis=-1).astype(jnp.bfloat16)
    yx = jnp.dot(mix, wout_ref[...], preferred_element_type=jnp.float32)
    mod = mod_ref[0]
    x1 = x_ref[0] + mod[2:3, :] * yx
    x1_out[0] = x1
    h2 = _rmsnorm_rows(x1, n2g_ref[...]) * (1.0 + mod[4:5, :]) + mod[3:4, :]
    h2_out[0] = h2.astype(jnp.bfloat16)

    lg = lax.dot_general(wr_ref[...], h2, (((1,), (1,)), ((), ())), precision=_HI,
                         preferred_element_type=jnp.float32) + br_ref[...]
    gl = lg[0:MOE_GROUPS, :]
    grow = lax.broadcasted_iota(jnp.int32, gl.shape, 0)
    gmax = jnp.max(gl, axis=0, keepdims=True)
    g_sel = jnp.min(jnp.where(gl == gmax, grow, MOE_GROUPS), axis=0, keepdims=True)
    p_group = 1.0 / jnp.sum(jnp.exp(gl - gmax), axis=0, keepdims=True)
    el = jnp.zeros((MOE_PER_GROUP, TOK_TILE), jnp.float32)
    for gi in range(MOE_GROUPS):
        el = jnp.where(g_sel == gi, lg[8 + gi * MOE_PER_GROUP:8 + (gi + 1) * MOE_PER_GROUP, :], el)
    erow = lax.broadcasted_iota(jnp.int32, el.shape, 0)
    m1 = jnp.max(el, axis=0, keepdims=True)
    i1 = jnp.min(jnp.where(el == m1, erow, MOE_PER_GROUP), axis=0, keepdims=True)
    el2 = jnp.where(erow == i1, -jnp.inf, el)
    m2 = jnp.max(el2, axis=0, keepdims=True)
    i2 = jnp.min(jnp.where(el2 == m2, erow, MOE_PER_GROUP), axis=0, keepdims=True)
    e21 = jnp.exp(m2 - m1)
    e0 = g_sel * MOE_PER_GROUP + i1
    e1 = g_sel * MOE_PER_GROUP + i2
    eid_out[0:1, :] = e0
    eid_out[1:2, :] = e1
    wt_out[0:1, :] = p_group / (1.0 + e21)
    wt_out[1:2, :] = p_group * e21 / (1.0 + e21)

    xrow = lax.broadcasted_iota(jnp.int32, (N_EXPERTS, TOK_TILE), 0)
    m0 = xrow == e0
    m1b = xrow == e1
    c0 = jnp.dot(jnp.where(m0, 1.0, 0.0).astype(jnp.bfloat16), uj_ref[...], preferred_element_type=jnp.float32)
    c1 = jnp.dot(jnp.where(m1b, 1.0, 0.0).astype(jnp.bfloat16), uj_ref[...], preferred_element_type=jnp.float32)
    run = run_ref[...]
    tot0 = c0[:, TOK_TILE:]
    rank_out[0:1, :] = jnp.sum(jnp.where(m0, run + c0[:, :TOK_TILE], 0.0), axis=0, keepdims=True).astype(jnp.int32)
    rank_out[1:2, :] = jnp.sum(jnp.where(m1b, run + tot0 + c1[:, :TOK_TILE], 0.0), axis=0,
                               keepdims=True).astype(jnp.int32)
    run = run + tot0 + c1[:, TOK_TILE:]
    run_ref[...] = run
    cnt_out[...] = run


def _finish_call(na_x, o_rw, bonus, g, x, mod3, ln_g, ln_b, w_out_bf16, norm2_g, bd, wr_t, br, uj):
    n_lat = SEQ // TOK_TILE
    full = lambda shape: pl.BlockSpec(shape, lambda b, j: (0,) * len(shape))
    lat = lambda width: pl.BlockSpec((1, TOK_TILE, width), lambda b, j: (b, j, 0))
    allt = lambda width: pl.BlockSpec((1, TOK_TILE, width), lambda b, j: (b, j + 1, 0))
    tokl = pl.BlockSpec((2, TOK_TILE), lambda b, j: (0, b * n_lat + j))
    return pl.pallas_call(
        _finish_kernel,
        out_shape=(jax.ShapeDtypeStruct((BATCH, SEQ, D_MODEL), jnp.float32),
                   jax.ShapeDtypeStruct((BATCH, SEQ, D_MODEL), jnp.bfloat16),
                   jax.ShapeDtypeStruct((2, N_TOK), jnp.int32),
                   jax.ShapeDtypeStruct((2, N_TOK), jnp.float32),
                   jax.ShapeDtypeStruct((2, N_TOK), jnp.int32),
                   jax.ShapeDtypeStruct((N_EXPERTS, TOK_TILE), jnp.float32)),
        grid=(BATCH, n_lat),
        in_specs=[lat(D_NA), lat(D_RW), allt(D_RW), allt(D_RW), lat(D_MODEL),
                  pl.BlockSpec((1, 6, D_MODEL), lambda b, j: (b, 0, 0)),
                  full((1, D_RW)), full((1, D_RW)), full((D_MODEL, D_MODEL)), full((1, D_MODEL)),
                  full((D_RW, D_RW)), full((LOGIT_ROWS, D_MODEL)), full((LOGIT_ROWS, 1)),
                  full((TOK_TILE, 2 * TOK_TILE))],
        out_specs=(lat(D_MODEL), lat(D_MODEL), tokl, tokl, tokl, full((N_EXPERTS, TOK_TILE))),
        scratch_shapes=[pltpu.VMEM((N_EXPERTS, TOK_TILE), jnp.float32)],
        compiler_params=pltpu.CompilerParams(dimension_semantics=("arbitrary", "arbitrary"),
                                             vmem_limit_bytes=VMEM_LIMIT),
        name="finish",
    )(na_x, o_rw, bonus, g, x, mod3, ln_g, ln_b, w_out_bf16, norm2_g, bd, wr_t, br, uj)


def _moe_kernel(te_ref, tv_ref, tx_ref, xs_ref, w1_ref, w3_ref, w2_ref, y_ref, w1s, w3s, w2s):
    t = pl.program_id(0)

    @pl.when((t == 0) | (te_ref[t] != te_ref[jnp.maximum(t - 1, 0)]))
    def _():
        w1s[...] = w1_ref[0].astype(jnp.bfloat16)
        w3s[...] = w3_ref[0].astype(jnp.bfloat16)
        w2s[...] = w2_ref[0].astype(jnp.bfloat16)

    @pl.when(tv_ref[t] == 1)
    def _():
        xs = xs_ref[...]
        h1 = jnp.dot(xs, w1s[...], preferred_element_type=jnp.float32)
        h3 = jnp.dot(xs, w3s[...], preferred_element_type=jnp.float32)
        he = (_silu(h1) * h3).astype(jnp.bfloat16)
        y_ref[...] = jnp.dot(he, w2s[...], preferred_element_type=jnp.float32).astype(jnp.bfloat16)

    @pl.when(tv_ref[t] == 0)
    def _():
        y_ref[...] = jnp.zeros_like(y_ref)


def _moe_call(tile_e, tile_valid, tile_src, xs, w1, w3, w2):
    return pl.pallas_call(
        _moe_kernel,
        out_shape=jax.ShapeDtypeStruct((N_SLOTS, D_MODEL), jnp.bfloat16),
        grid_spec=pltpu.PrefetchScalarGridSpec(
            num_scalar_prefetch=3,
            grid=(N_MOE_TILES,),
            in_specs=[pl.BlockSpec((MOE_TILE, D_MODEL), lambda t, te, tv, tx: (tx[t], 0)),
                      pl.BlockSpec((1, D_MODEL, D_EXPERT), lambda t, te, tv, tx: (te[t], 0, 0)),
                      pl.BlockSpec((1, D_MODEL, D_EXPERT), lambda t, te, tv, tx: (te[t], 0, 0)),
                      pl.BlockSpec((1, D_EXPERT, D_MODEL), lambda t, te, tv, tx: (te[t], 0, 0))],
            out_specs=pl.BlockSpec((MOE_TILE, D_MODEL), lambda t, te, tv, tx: (t, 0)),
            scratch_shapes=[pltpu.VMEM((D_MODEL, D_EXPERT), jnp.bfloat16),
                            pltpu.VMEM((D_MODEL, D_EXPERT), jnp.bfloat16),
                            pltpu.VMEM((D_EXPERT, D_MODEL), jnp.bfloat16)]),
        compiler_params=pltpu.CompilerParams(dimension_semantics=("arbitrary",), vmem_limit_bytes=VMEM_LIMIT),
        name="moe",
    )(tile_e, tile_valid, tile_src, xs, w1, w3, w2)


def _moe_plan(eid, rank, cnt):
    counts = cnt[:, 0].astype(jnp.int32)
    gsz = ((counts + MOE_TILE - 1) // MOE_TILE) * MOE_TILE
    gend = jnp.cumsum(gsz)
    goff = gend - gsz
    pos = rank
    for e in range(N_EXPERTS):
        pos = pos + jnp.where(eid == e, goff[e], 0)
    starts = jnp.arange(N_MOE_TILES, dtype=jnp.int32) * MOE_TILE
    te = jnp.sum((gend[None, :] <= starts[:, None]).astype(jnp.int32), axis=1)
    valid = (te < N_EXPERTS).astype(jnp.int32)
    last = jnp.max(jnp.where(counts > 0, jnp.arange(N_EXPERTS), 0)).astype(jnp.int32)
    te = jnp.where(valid == 1, te, last)
    n_valid = gend[-1] // MOE_TILE
    tx = jnp.minimum(jnp.arange(N_MOE_TILES, dtype=jnp.int32), n_valid - 1)
    tok = jnp.broadcast_to(jnp.arange(N_TOK, dtype=jnp.int32), (2, N_TOK))
    src_tok = jnp.zeros((N_SLOTS,), jnp.int32).at[pos.reshape(-1)].set(tok.reshape(-1))
    return pos, src_tok, te, valid, tx


def _final_kernel(x1_ref, ya_ref, yb_ref, wt_ref, mod_ref, o_ref):
    w = wt_ref[...]
    ya = ya_ref[0].astype(jnp.float32)
    yb = yb_ref[0].astype(jnp.float32)
    o_ref[0] = x1_ref[0] + mod_ref[0][5:6, :] * (w[:, 0:1] * ya + w[:, 1:2] * yb)


def _final_call(x1, ya, yb, wt_rows, mod3):
    n_lat = SEQ // TOK_TILE
    lat = pl.BlockSpec((1, TOK_TILE, D_MODEL), lambda b, j: (b, j, 0))
    return pl.pallas_call(
        _final_kernel,
        out_shape=jax.ShapeDtypeStruct((BATCH, SEQ, D_MODEL), jnp.float32),
        grid=(BATCH, n_lat),
        in_specs=[lat, lat, lat, pl.BlockSpec((TOK_TILE, 2), lambda b, j: (b * n_lat + j, 0)),
                  pl.BlockSpec((1, 6, D_MODEL), lambda b, j: (b, 0, 0))],
        out_specs=lat,
        name="final",
    )(x1, ya, yb, wt_rows, mod3)


def _block_diag2(w):
    z = jnp.zeros_like(w[0])
    return jnp.concatenate([jnp.concatenate([w[0], z], axis=1), jnp.concatenate([z, w[1]], axis=1)], axis=0)


def kernel(x, c, ctx, c_ctx, w_mod, b_mod, norm1_g, norm2_g, w_in, na_q_g, na_k_g, na_rpb, rw_mu_prev, rw_mu_next,
           rw_w0, rw_w_up, rw_a0, rw_a_up, rw_g_up, rw_k_k, rw_k_a, rw_r_k, rw_ln_g, rw_ln_b, w_out, moe_wg, moe_bg,
           moe_we, moe_be, moe_w1, moe_w3, moe_w2):
    bf = jnp.bfloat16
    mod_rows = BATCH + 8
    cs = jnp.concatenate([c, c_ctx[None, :], jnp.zeros((mod_rows - BATCH - 1, D_MODEL), jnp.float32)], axis=0)
    mod = _mod_call(cs, w_mod[0], b_mod[0][None, :])
    mod3 = mod.reshape(mod_rows, 6, D_MODEL)

    head = jnp.arange(D_RW) // HEAD_DIM
    bd = (head[:, None] == head[None, :]).astype(bf)
    qkv, p_rw = _inproj_call(x, ctx, mod3, norm1_g[0][None, :], w_in[0].astype(bf),
                             jnp.tile(na_q_g[0], H_NA)[None, :], jnp.tile(na_k_g[0], H_NA)[None, :], bd)

    na_x = _na_call(qkv, _na_bias_table(na_rpb[0]))

    cos_t, sin_t = _rope_tables()
    p_rw_t = p_rw.transpose(1, 0, 2).reshape(T_ALL * BATCH, D_RW_IN)
    v, abar, rbar, ktil, btil, gend, g_t, bonus_t = _prep_call(
        p_rw_t, cos_t, sin_t, rw_mu_prev[0][None, :], rw_mu_next[0][None, :],
        rw_w0[0].reshape(1, 2 * D_RW), _block_diag2(rw_w_up[0]).astype(bf),
        rw_a0[0].reshape(1, 2 * D_RW), _block_diag2(rw_a_up[0]).astype(bf),
        rw_g_up[0].astype(bf), rw_k_k[0][None, :], rw_k_a[0][None, :], rw_r_k[0].reshape(1, D_RW), bd)
    g = g_t.reshape(T_ALL, BATCH, D_RW).transpose(1, 0, 2)
    bonus = bonus_t.reshape(T_ALL, BATCH, D_RW).transpose(1, 0, 2)

    o2 = _scan_call(v, abar, rbar, ktil, btil, gend)
    o_rw = _from_scan_layout(o2[0] + o2[1])

    wr_t = jnp.zeros((LOGIT_ROWS, D_MODEL), jnp.float32)
    wr_t = wr_t.at[0:MOE_GROUPS].set(moe_wg[0].T)
    wr_t = wr_t.at[8:].set(moe_we[0].transpose(0, 2, 1).reshape(N_EXPERTS, D_MODEL))
    br = jnp.zeros((LOGIT_ROWS,), jnp.float32).at[0:MOE_GROUPS].set(moe_bg[0]).at[8:].set(moe_be[0].reshape(-1))
    tri = jnp.arange(TOK_TILE)
    uj = jnp.concatenate([(tri[:, None] < tri[None, :]).astype(bf), jnp.ones((TOK_TILE, TOK_TILE), bf)], axis=1)
    x1, h2p, eid, wts, rank, cnt = _finish_call(na_x, o_rw, bonus, g, x, mod3, rw_ln_g[0][None, :],
                                                rw_ln_b[0][None, :], w_out[0].astype(bf), norm2_g[0][None, :], bd,
                                                wr_t, br[:, None], uj)

    pos, src_tok, tile_e, tile_valid, tile_src = _moe_plan(eid, rank, cnt)
    xs = jnp.take(h2p.reshape(N_TOK, D_MODEL), src_tok, axis=0)
    ys = _moe_call(tile_e, tile_valid, tile_src, xs, moe_w1[0], moe_w3[0], moe_w2[0])
    ya = jnp.take(ys, pos[0], axis=0).reshape(BATCH, SEQ, D_MODEL)
    yb = jnp.take(ys, pos[1], axis=0).reshape(BATCH, SEQ, D_MODEL)
    return _final_call(x1, ya, yb, wts.T, mod3)
```

```python
import functools
import math

import jax
import jax.numpy as jnp
import numpy as np
from jax import lax
from jax.experimental import pallas as pl
from jax.experimental.pallas import tpu as pltpu

D_MODEL = 1024
BATCH = 16
SEQ = 2048
GRID_W = 64
GRID_H = SEQ // GRID_W
CTX_LEN = 256
T_ALL = CTX_LEN + SEQ
HEAD_DIM = 64
D_NA = 512
D_RW = 512
H_NA = D_NA // HEAD_DIM
H_RW = D_RW // HEAD_DIM
NA_KH = 8
NA_KW = 16
LORA_W = 64
LORA_A = 64
LORA_G = 128
D_RW_IN = 3 * D_RW + 2 * (LORA_W + LORA_A) + LORA_G
D_IN = 3 * D_NA + D_RW_IN
MOE_GROUPS = 4
MOE_PER_GROUP = 8
N_EXPERTS = MOE_GROUPS * MOE_PER_GROUP
D_EXPERT = 512
ROPE_THETA = 10000.0
NORM_EPS = 1e-6
RW_LN_EPS = 64e-5
NEG = -1e30
DECAY_SCALE = math.exp(-0.5)

LANES = 128
TOK_TILE = 256
FIN_TILE = 512
N_TILES = T_ALL // TOK_TILE
SCAN_CHUNK = 16
MOE_TILE = 256
N_TOK = BATCH * SEQ
N_SLOTS = 2 * N_TOK + N_EXPERTS * MOE_TILE
N_MOE_TILES = N_SLOTS // MOE_TILE
LOGIT_ROWS = 8 + N_EXPERTS
VMEM_LIMIT = 48 * 1024 * 1024

_HI = lax.Precision.HIGHEST


def _sigmoid(x):
    return 1.0 / (1.0 + jnp.exp(-x))


def _silu(x):
    return x * _sigmoid(x)


def _rmsnorm_rows(xf, g):
    return xf * lax.rsqrt(jnp.mean(xf * xf, axis=-1, keepdims=True) + NORM_EPS) * g


def _group_sum(x, bd):
    hi = x.astype(jnp.bfloat16)
    lo = (x - hi.astype(jnp.float32)).astype(jnp.bfloat16)
    return (jnp.dot(hi, bd, preferred_element_type=jnp.float32)
            + jnp.dot(lo, bd, preferred_element_type=jnp.float32))


def _mod_kernel(c_ref, w_ref, b_ref, o_ref):
    o_ref[...] = jnp.dot(_silu(c_ref[...]), w_ref[...], precision=_HI,
                         preferred_element_type=jnp.float32) + b_ref[...]


def _mod_call(cs, w_mod, b_mod):
    rows = cs.shape[0]
    n = w_mod.shape[1]
    blk = 1024
    return pl.pallas_call(
        _mod_kernel,
        out_shape=jax.ShapeDtypeStruct((rows, n), jnp.float32),
        grid=(n // blk,),
        in_specs=[pl.BlockSpec((rows, D_MODEL), lambda j: (0, 0)),
                  pl.BlockSpec((D_MODEL, blk), lambda j: (0, j)),
                  pl.BlockSpec((1, blk), lambda j: (0, j))],
        out_specs=pl.BlockSpec((rows, blk), lambda j: (0, j)),
        name="mod",
    )(cs, w_mod, b_mod)


def _inproj_kernel(x_ref, ctx_ref, modx_ref, modc_ref, g_ref, w_ref, qg_ref, kg_ref, bd_ref, na_ref, rw_ref):
    is_ctx = pl.program_id(1) == 0
    xin = jnp.where(is_ctx, ctx_ref[0], x_ref[0])
    mod = jnp.where(is_ctx, modc_ref[0], modx_ref[0])
    h = _rmsnorm_rows(xin, g_ref[...]) * (1.0 + mod[1:2, :]) + mod[0:1, :]
    p = jnp.dot(h.astype(jnp.bfloat16), w_ref[...], preferred_element_type=jnp.float32)
    rw_ref[0] = p[:, 3 * D_NA:]
    bd = bd_ref[...]
    q = p[:, 0:D_NA]
    k = p[:, D_NA:2 * D_NA]
    qn = q * lax.rsqrt(_group_sum(q * q, bd) * (1.0 / HEAD_DIM) + NORM_EPS) * (qg_ref[...] * HEAD_DIM ** -0.5)
    kn = k * lax.rsqrt(_group_sum(k * k, bd) * (1.0 / HEAD_DIM) + NORM_EPS) * kg_ref[...]
    na_ref[0, :, 0:D_NA] = qn.astype(jnp.bfloat16)
    na_ref[0, :, D_NA:2 * D_NA] = kn.astype(jnp.bfloat16)
    na_ref[0, :, 2 * D_NA:] = p[:, 2 * D_NA:3 * D_NA].astype(jnp.bfloat16)


def _inproj_call(x, ctx, mod3, norm_g, w_in_bf16, q_g, k_g, bd):
    return pl.pallas_call(
        _inproj_kernel,
        out_shape=(jax.ShapeDtypeStruct((BATCH, T_ALL, 3 * D_NA), jnp.bfloat16),
                   jax.ShapeDtypeStruct((BATCH, T_ALL, D_RW_IN), jnp.float32)),
        grid=(BATCH, N_TILES),
        in_specs=[pl.BlockSpec((1, TOK_TILE, D_MODEL), lambda b, j: (b, jnp.maximum(j - 1, 0), 0)),
                  pl.BlockSpec((1, TOK_TILE, D_MODEL), lambda b, j: (b, 0, 0)),
                  pl.BlockSpec((1, 6, D_MODEL), lambda b, j: (b, 0, 0)),
                  pl.BlockSpec((1, 6, D_MODEL), lambda b, j: (BATCH, 0, 0)),
                  pl.BlockSpec((1, D_MODEL), lambda b, j: (0, 0)),
                  pl.BlockSpec((D_MODEL, D_IN), lambda b, j: (0, 0)),
                  pl.BlockSpec((1, D_NA), lambda b, j: (0, 0)),
                  pl.BlockSpec((1, D_NA), lambda b, j: (0, 0)),
                  pl.BlockSpec((D_NA, D_NA), lambda b, j: (0, 0))],
        out_specs=(pl.BlockSpec((1, TOK_TILE, 3 * D_NA), lambda b, j: (b, j, 0)),
                   pl.BlockSpec((1, TOK_TILE, D_RW_IN), lambda b, j: (b, j, 0))),
        compiler_params=pltpu.CompilerParams(vmem_limit_bytes=VMEM_LIMIT),
        name="inproj",
    )(x, ctx, mod3, mod3, norm_g, w_in_bf16, q_g, k_g, bd)


NA_QROWS = 4
NA_QBLK = NA_QROWS * GRID_W
NA_BAND = NA_KH + NA_QROWS - 1
NA_BAND_KEYS = NA_BAND * GRID_W
NA_NBLK = GRID_H // NA_QROWS


def _na_band_start(i):
    return np.clip(i * NA_QROWS - NA_KH // 2, 0, GRID_H - NA_BAND)


def _na_kernel(q_ref, k_ref, v_ref, tz_ref, o_ref, bias_ref):
    @pl.when(pl.program_id(1) == 0)
    def _():
        masked = jnp.full((GRID_W, GRID_W), NEG, jnp.float32)
        for h in range(2):
            for p, i in enumerate((0, 1, NA_NBLK - 1)):
                bs = int(_na_band_start(i))
                for g in range(NA_QROWS):
                    qr = i * NA_QROWS + g
                    rs = int(np.clip(qr - NA_KH // 2, 0, GRID_H - NA_KH))
                    for j in range(NA_BAND):
                        kr = bs + j
                        blk = tz_ref[h, kr - qr + NA_KH - 1] if rs <= kr < rs + NA_KH else masked
                        bias_ref[h, p, g * GRID_W:(g + 1) * GRID_W, j * GRID_W:(j + 1) * GRID_W] = blk

    nt = (((1,), (1,)), ((), ()))
    head0 = lax.broadcasted_iota(jnp.int32, (NA_QBLK, LANES), 1) < HEAD_DIM

    def body(i, carry):
        bs = jnp.clip(i * NA_QROWS - NA_KH // 2, 0, GRID_H - NA_BAND)
        pattern = jnp.where(i == 0, 0, jnp.where(i == NA_NBLK - 1, 2, 1))
        q0 = pl.multiple_of(i * NA_QBLK, NA_QBLK)
        k0 = pl.multiple_of(CTX_LEN + bs * GRID_W, GRID_W)
        q2 = q_ref[0, pl.ds(CTX_LEN + q0, NA_QBLK), :]
        kw = k_ref[0, pl.ds(k0, NA_BAND_KEYS), :]
        vw = v_ref[0, pl.ds(k0, NA_BAND_KEYS), :]
        kc = k_ref[0, 0:CTX_LEN, :]
        vc = v_ref[0, 0:CTX_LEN, :]
        outs = []
        for h in range(2):
            qb = jnp.where(head0 if h == 0 else ~head0, q2, jnp.zeros_like(q2))
            s_win = lax.dot_general(qb, kw, nt, preferred_element_type=jnp.float32) + bias_ref[h, pattern]
            s_ctx = lax.dot_general(qb, kc, nt, preferred_element_type=jnp.float32)
            m = jnp.maximum(jnp.max(s_win, axis=-1, keepdims=True), jnp.max(s_ctx, axis=-1, keepdims=True))
            e_win = jnp.exp(s_win - m)
            e_ctx = jnp.exp(s_ctx - m)
            den = jnp.sum(e_win, axis=-1, keepdims=True) + jnp.sum(e_ctx, axis=-1, keepdims=True)
            o = (jnp.dot(e_win.astype(jnp.bfloat16), vw, preferred_element_type=jnp.float32)
                 + jnp.dot(e_ctx.astype(jnp.bfloat16), vc, preferred_element_type=jnp.float32))
            outs.append(o / den)
        o_ref[0, pl.ds(q0, NA_QBLK), :] = jnp.where(head0, outs[0], outs[1])
        return carry

    lax.fori_loop(0, NA_NBLK, body, 0)


def _na_call(qkv, bias8):
    n_hp = D_NA // LANES
    return pl.pallas_call(
        _na_kernel,
        out_shape=jax.ShapeDtypeStruct((BATCH, SEQ, D_NA), jnp.float32),
        grid=(n_hp, BATCH),
        in_specs=[pl.BlockSpec((1, T_ALL, LANES), lambda hp, b: (b, 0, hp)),
                  pl.BlockSpec((1, T_ALL, LANES), lambda hp, b: (b, 0, n_hp + hp)),
                  pl.BlockSpec((1, T_ALL, LANES), lambda hp, b: (b, 0, 2 * n_hp + hp)),
                  pl.BlockSpec((2, 2 * NA_KH - 1, GRID_W, GRID_W), lambda hp, b: (hp, 0, 0, 0))],
        out_specs=pl.BlockSpec((1, SEQ, LANES), lambda hp, b: (b, 0, hp)),
        scratch_shapes=[pltpu.VMEM((2, 3, NA_QBLK, NA_BAND_KEYS), jnp.float32)],
        compiler_params=pltpu.CompilerParams(dimension_semantics=("arbitrary", "arbitrary"),
                                             vmem_limit_bytes=VMEM_LIMIT),
        name="na",
    )(qkv, qkv, qkv, bias8)


def _na_bias_table(rpb):
    qc = np.arange(GRID_W)
    cs = np.clip(qc - NA_KW // 2, 0, GRID_W - NA_KW)
    kc = np.arange(GRID_W)
    col_ok = (kc[None, :] >= cs[:, None]) & (kc[None, :] < cs[:, None] + NA_KW)
    dc = np.clip(kc[None, :] - qc[:, None], -(NA_KW - 1), NA_KW - 1) + NA_KW - 1
    return jnp.where(col_ok[None, None], rpb[:, :, dc], NEG).astype(jnp.float32)


def _swap16(x):
    lane = lax.broadcasted_iota(jnp.int32, x.shape, 1)
    return jnp.where((lane & 16) == 0, pltpu.roll(x, LANES - 16, axis=1), pltpu.roll(x, 16, axis=1))


def _rope(x, cos, sin):
    blocks = []
    for i in range(x.shape[1] // LANES):
        xb = x[:, i * LANES:(i + 1) * LANES]
        blocks.append(xb * cos + _swap16(xb) * sin)
    return jnp.concatenate(blocks, axis=1)


def _shift_rows(x, n, fill, down):
    if down:
        return jnp.concatenate([fill, x[:-n]], axis=0)
    return jnp.concatenate([x[n:], fill], axis=0)


def _chunk_cumprod(w, reverse):
    inc = w
    s = 1
    while s < SCAN_CHUNK:
        ones = jnp.ones((s * BATCH, w.shape[1]), jnp.float32)
        inc = inc * _shift_rows(inc, s * BATCH, ones, down=not reverse)
        s *= 2
    exc = _shift_rows(inc, BATCH, jnp.ones((BATCH, w.shape[1]), jnp.float32), down=not reverse)
    return inc, exc


def _scan_tiles(u):
    low = lax.broadcasted_iota(jnp.int32, (BATCH, LANES), 1) < HEAD_DIM
    steps = u.shape[0] // BATCH
    tiles = []
    for m in range(0, steps, 2):
        a = u[m * BATCH:(m + 1) * BATCH]
        b = u[(m + 1) * BATCH:(m + 2) * BATCH] if m + 1 < steps else a
        rows = []
        for hp in range(D_RW // LANES):
            pa = a[:, hp * LANES:(hp + 1) * LANES]
            pb = b[:, hp * LANES:(hp + 1) * LANES]
            rows.append(jnp.where(low, pa, pltpu.roll(pb, HEAD_DIM, axis=1)))
            rows.append(jnp.where(low, pltpu.roll(pa, HEAD_DIM, axis=1), pb))
        wt = jnp.concatenate(rows, axis=0).T
        tiles.append(wt[:HEAD_DIM])
        if m + 1 < steps:
            tiles.append(wt[HEAD_DIM:])
    return tiles


def _prep_kernel(p_ref, pv_ref, nx_ref, cos_ref, sin_ref, mup_ref, mun_ref, w0_ref, wup_ref, a0_ref, aup_ref,
                 gup_ref, kk_ref, ka_ref, rk_ref, bd_ref,
                 v_out, abar_out, rbar_out, ktil_out, btil_out, gend_out, g_out, bonus_out):
    j = pl.program_id(0)
    n_ctx = CTX_LEN // SCAN_CHUNK
    p = p_ref[...]
    seq_start = (j == 0) | (j == n_ctx)
    seq_end = (j == n_ctx - 1) | (j == pl.num_programs(0) - 1)
    prev = _shift_rows(p, BATCH, jnp.where(seq_start, 0.0, pv_ref[...]), down=True)
    nxt = _shift_rows(p, BATCH, jnp.where(seq_end, 0.0, nx_ref[...]), down=False)
    ps = p + mup_ref[...] * (prev - p) + mun_ref[...] * (nxt - p)

    cos = cos_ref[...]
    sin = sin_ref[...]
    r = _rope(ps[:, 0:D_RW], cos, sin)
    k = _rope(ps[:, D_RW:2 * D_RW], cos, sin)
    v = ps[:, 2 * D_RW:3 * D_RW]
    o3 = 3 * D_RW
    wd = ps[:, o3:o3 + 2 * LORA_W]
    ad = ps[:, o3 + 2 * LORA_W:o3 + 2 * LORA_W + 2 * LORA_A]
    gd = ps[:, o3 + 2 * LORA_W + 2 * LORA_A:]

    w_pre = w0_ref[...] + jnp.dot(jnp.tanh(wd).astype(jnp.bfloat16), wup_ref[...], preferred_element_type=jnp.float32)
    decay = jnp.exp(-DECAY_SCALE * _sigmoid(w_pre))
    a = _sigmoid(a0_ref[...] + jnp.dot(ad.astype(jnp.bfloat16), aup_ref[...], preferred_element_type=jnp.float32))
    g_out[...] = jnp.dot(_sigmoid(gd).astype(jnp.bfloat16), gup_ref[...], preferred_element_type=jnp.float32)

    bd = bd_ref[...]
    kk = k * kk_ref[...]
    kk = kk / jnp.maximum(jnp.sqrt(_group_sum(kk * kk, bd)), 1e-12)
    bonus_out[...] = _group_sum(r * k * rk_ref[...], bd) * v
    ka = ka_ref[...]

    for t, tile in enumerate(_scan_tiles(v)):
        v_out[t] = tile
    for d in range(2):
        ds_ = slice(d * D_RW, (d + 1) * D_RW)
        a_d = a[:, ds_]
        gam, gam_excl = _chunk_cumprod(decay[:, ds_], reverse=(d == 1))
        inv = 1.0 / gam
        streams = ((abar_out, gam_excl * kk), (rbar_out, gam * r),
                   (ktil_out, k * (1.0 + (a_d - 1.0) * ka) * inv), (btil_out, kk * a_d * inv))
        for out, val in streams:
            for t, tile in enumerate(_scan_tiles(val)):
                out[d, t] = tile
        gam_last = gam[:BATCH] if d == 1 else gam[(SCAN_CHUNK - 1) * BATCH:]
        gend_out[d, 0] = _scan_tiles(gam_last)[0]


def _prep_call(p_rw_t, cos_t, sin_t, mu_prev, mu_next, w0, wup2, a0, aup2, gup, k_k, k_a, r_k, bd):
    rows = SCAN_CHUNK * BATCH
    n_chunks = T_ALL // SCAN_CHUNK
    full = lambda shape: pl.BlockSpec(shape, lambda j: (0,) * len(shape))
    tok = lambda width: pl.BlockSpec((rows, width), lambda j: (j, 0))
    perdir = pl.BlockSpec((2, SCAN_CHUNK, HEAD_DIM, LANES), lambda j: (0, j, 0, 0))
    two = jax.ShapeDtypeStruct((2, T_ALL, HEAD_DIM, LANES), jnp.float32)
    nat = jax.ShapeDtypeStruct((T_ALL * BATCH, D_RW), jnp.float32)
    return pl.pallas_call(
        _prep_kernel,
        out_shape=(jax.ShapeDtypeStruct((T_ALL, HEAD_DIM, LANES), jnp.float32), two, two, two, two,
                   jax.ShapeDtypeStruct((2, n_chunks, HEAD_DIM, LANES), jnp.float32), nat, nat),
        grid=(n_chunks,),
        in_specs=[tok(D_RW_IN),
                  pl.BlockSpec((BATCH, D_RW_IN), lambda j: (jnp.maximum(j * SCAN_CHUNK - 1, 0), 0)),
                  pl.BlockSpec((BATCH, D_RW_IN), lambda j: (jnp.minimum((j + 1) * SCAN_CHUNK, T_ALL - 1), 0)),
                  tok(LANES), tok(LANES),
                  full((1, D_RW_IN)), full((1, D_RW_IN)),
                  full((1, 2 * D_RW)), full((2 * LORA_W, 2 * D_RW)),
                  full((1, 2 * D_RW)), full((2 * LORA_A, 2 * D_RW)),
                  full((LORA_G, D_RW)),
                  full((1, D_RW)), full((1, D_RW)), full((1, D_RW)),
                  full((D_RW, D_RW))],
        out_specs=(pl.BlockSpec((SCAN_CHUNK, HEAD_DIM, LANES), lambda j: (j, 0, 0)), perdir, perdir, perdir, perdir,
                   pl.BlockSpec((2, 1, HEAD_DIM, LANES), lambda j: (0, j, 0, 0)), tok(D_RW), tok(D_RW)),
        compiler_params=pltpu.CompilerParams(vmem_limit_bytes=VMEM_LIMIT),
        name="prep",
    )(p_rw_t, p_rw_t, p_rw_t, cos_t, sin_t, mu_prev, mu_next, w0, wup2, a0, aup2, gup, k_k, k_a, r_k, bd)


def _rope_tables():
    nf = HEAD_DIM // 4
    pos = np.arange(SEQ)
    inv = ROPE_THETA ** (-np.arange(nf, dtype=np.float32) / nf)
    lane = np.arange(LANES) % HEAD_DIM
    half, pair, f = lane // 32, (lane % 32) // 16, lane % 16
    coord = np.where(half[None, :] == 0, (pos // GRID_W)[:, None], (pos % GRID_W)[:, None]).astype(np.float32)
    ang = coord * inv[f][None, :].astype(np.float32)
    cos = np.cos(ang).astype(np.float32)
    sin = np.sin(ang).astype(np.float32) * np.where(pair == 0, -1.0, 1.0)[None, :].astype(np.float32)
    cos = np.concatenate([np.ones((CTX_LEN, LANES), np.float32), cos], axis=0)
    sin = np.concatenate([np.zeros((CTX_LEN, LANES), np.float32), sin], axis=0)
    return jnp.asarray(np.repeat(cos, BATCH, axis=0)), jnp.asarray(np.repeat(sin, BATCH, axis=0))


def _scan_kernel(v_ref, abar_ref, rbar_ref, ktil_ref, btil_ref, gend_ref, o_ref, s_ref):
    d = pl.program_id(0)
    i = pl.program_id(1)

    @pl.when(i == 0)
    def _():
        s_ref[...] = jnp.zeros_like(s_ref)

    k_unroll = 32
    n_kb = HEAD_DIM // k_unroll
    zero = jnp.zeros((HEAD_DIM, LANES), jnp.float32)

    def make_step(with_out):
        def step(s, carry):
            t = jnp.where(d == 0, s, SCAN_CHUNK - 1 - s)
            vv = v_ref[t]

            def sk_body(kb, sk):
                for u in range(k_unroll):
                    k = kb * k_unroll + u
                    sk = sk + s_ref[k] * abar_ref[0, t, pl.ds(k, 1), :]
                return sk

            sk = lax.fori_loop(0, n_kb, sk_body, zero)

            def upd_body(kb, o):
                for u in range(k_unroll):
                    k = kb * k_unroll + u
                    s_new = s_ref[k] + (vv * ktil_ref[0, t, pl.ds(k, 1), :] - sk * btil_ref[0, t, pl.ds(k, 1), :])
                    s_ref[k] = s_new
                    if with_out:
                        o = o + s_new * rbar_ref[0, t, pl.ds(k, 1), :]
                return o

            o = lax.fori_loop(0, n_kb, upd_body, zero)
            if with_out:
                o_ref[0, t] = o
            return carry
        return step

    @pl.when(i < CTX_LEN // SCAN_CHUNK)
    def _():
        lax.fori_loop(0, SCAN_CHUNK, make_step(False), 0)

    @pl.when(i >= CTX_LEN // SCAN_CHUNK)
    def _():
        lax.fori_loop(0, SCAN_CHUNK, make_step(True), 0)

    def renorm(kb, carry):
        for u in range(k_unroll):
            k = kb * k_unroll + u
            s_ref[k] = s_ref[k] * gend_ref[0, 0, pl.ds(k, 1), :]
        return carry

    lax.fori_loop(0, n_kb, renorm, 0)


def _scan_call(v_t, abar_t, rbar_t, ktil_t, btil_t, gend_t):
    nc = CTX_LEN // SCAN_CHUNK
    nl = SEQ // SCAN_CHUNK

    def blk(d, i):
        rev = jnp.where(i < nc, nc - 1 - i, 2 * nc + nl - 1 - i)
        return jnp.where(d == 0, i, rev)

    def oblk(d, i):
        fwd = jnp.maximum(i - nc, 0)
        rev = jnp.where(i < nc, nl - 1, nc + nl - 1 - i)
        return jnp.where(d == 0, fwd, rev)

    shared = pl.BlockSpec((SCAN_CHUNK, HEAD_DIM, LANES), lambda d, i: (blk(d, i), 0, 0))
    perdir = pl.BlockSpec((1, SCAN_CHUNK, HEAD_DIM, LANES), lambda d, i: (d, blk(d, i), 0, 0))
    perchunk = pl.BlockSpec((1, 1, HEAD_DIM, LANES), lambda d, i: (d, blk(d, i), 0, 0))
    return pl.pallas_call(
        _scan_kernel,
        out_shape=jax.ShapeDtypeStruct((2, SEQ, HEAD_DIM, LANES), jnp.float32),
        grid=(2, nc + nl),
        in_specs=[shared, perdir, perdir, perdir, perdir, perchunk],
        out_specs=pl.BlockSpec((1, SCAN_CHUNK, HEAD_DIM, LANES), lambda d, i: (d, oblk(d, i), 0, 0)),
        scratch_shapes=[pltpu.VMEM((HEAD_DIM, HEAD_DIM, LANES), jnp.float32)],
        compiler_params=pltpu.CompilerParams(dimension_semantics=("arbitrary", "arbitrary"),
                                             vmem_limit_bytes=VMEM_LIMIT),
        name="scan",
    )(v_t, abar_t, rbar_t, ktil_t, btil_t, gend_t)


def _from_scan_layout(o):
    return o.reshape(SEQ, HEAD_DIM, H_RW, BATCH).transpose(3, 0, 2, 1).reshape(BATCH, SEQ, D_RW)


def _finish_kernel(na_ref, o_ref, bonus_ref, g_ref, x_ref, mod_ref, lng_ref, lnb_ref, wout_ref, n2g_ref, bd_ref,
                   wr_ref, br_ref, uj_ref, x1_out, h2_out, eid_out, wt_out, rank_out, cnt_out, run_ref):
    @pl.when((pl.program_id(0) == 0) & (pl.program_id(1) == 0))
    def _():
        run_ref[...] = jnp.zeros_like(run_ref)

    bd = bd_ref[...]
    o = o_ref[0]
    mu = _group_sum(o, bd) * (1.0 / HEAD_DIM)
    oc = o - mu
    var = _group_sum(oc * oc, bd) * (1.0 / HEAD_DIM)
    y = oc * lax.rsqrt(var + RW_LN_EPS) * lng_ref[...] + lnb_ref[...]
    rw = (y + bonus_ref[0]) * g_ref[0]
    mix = jnp.concatenate([na_ref[0], rw], axis=-1).astype(jnp.bfloat16)
    yx = jnp.dot(mix, wout_ref[...], preferred_element_type=jnp.float32)
    mod = mod_ref[0]
    x1 = x_ref[0] + mod[2:3, :] * yx
    x1_out[0] = x1
    h2 = _rmsnorm_rows(x1, n2g_ref[...]) * (1.0 + mod[4:5, :]) + mod[3:4, :]
    h2_out[0] = h2.astype(jnp.bfloat16)

    lg = lax.dot_general(wr_ref[...], h2, (((1,), (1,)), ((), ())), precision=_HI,
                         preferred_element_type=jnp.float32) + br_ref[...]
    gl = lg[0:MOE_GROUPS, :]
    grow = lax.broadcasted_iota(jnp.int32, gl.shape, 0)
    gmax = jnp.max(gl, axis=0, keepdims=True)
    g_sel = jnp.min(jnp.where(gl == gmax, grow, MOE_GROUPS), axis=0, keepdims=True)
    p_group = 1.0 / jnp.sum(jnp.exp(gl - gmax), axis=0, keepdims=True)
    el = jnp.zeros((MOE_PER_GROUP, FIN_TILE), jnp.float32)
    for gi in range(MOE_GROUPS):
        el = jnp.where(g_sel == gi, lg[8 + gi * MOE_PER_GROUP:8 + (gi + 1) * MOE_PER_GROUP, :], el)
    erow = lax.broadcasted_iota(jnp.int32, el.shape, 0)
    m1 = jnp.max(el, axis=0, keepdims=True)
    i1 = jnp.min(jnp.where(el == m1, erow, MOE_PER_GROUP), axis=0, keepdims=True)
    el2 = jnp.where(erow == i1, -jnp.inf, el)
    m2 = jnp.max(el2, axis=0, keepdims=True)
    i2 = jnp.min(jnp.where(el2 == m2, erow, MOE_PER_GROUP), axis=0, keepdims=True)
    e21 = jnp.exp(m2 - m1)
    e0 = g_sel * MOE_PER_GROUP + i1
    e1 = g_sel * MOE_PER_GROUP + i2
    eid_out[0:1, :] = e0
    eid_out[1:2, :] = e1
    wt_out[0:1, :] = p_group / (1.0 + e21)
    wt_out[1:2, :] = p_group * e21 / (1.0 + e21)

    xrow = lax.broadcasted_iota(jnp.int32, (N_EXPERTS, FIN_TILE), 0)
    m0 = xrow == e0
    m1b = xrow == e1
    c0 = jnp.dot(jnp.where(m0, 1.0, 0.0).astype(jnp.bfloat16), uj_ref[...], preferred_element_type=jnp.float32)
    c1 = jnp.dot(jnp.where(m1b, 1.0, 0.0).astype(jnp.bfloat16), uj_ref[...], preferred_element_type=jnp.float32)
    run = run_ref[...]
    tot0 = c0[:, FIN_TILE:]
    rank_out[0:1, :] = jnp.sum(jnp.where(m0, run + c0[:, :FIN_TILE], 0.0), axis=0, keepdims=True).astype(jnp.int32)
    rank_out[1:2, :] = jnp.sum(jnp.where(m1b, run + tot0 + c1[:, :FIN_TILE], 0.0), axis=0,
                               keepdims=True).astype(jnp.int32)
    run = run + tot0 + c1[:, FIN_TILE:]
    run_ref[...] = run
    cnt_out[...] = run

def _finish_call(na_x, o_rw, bonus, g, x, mod3, ln_g, ln_b, w_out_bf16, norm2_g, bd, wr_t, br, uj):
    n_lat = SEQ // FIN_TILE
    full = lambda shape: pl.BlockSpec(shape, lambda b, j: (0,) * len(shape))
    lat = lambda width: pl.BlockSpec((1, FIN_TILE, width), lambda b, j: (b, j, 0))
    tokl = pl.BlockSpec((2, FIN_TILE), lambda b, j: (0, b * n_lat + j))
    return pl.pallas_call(
        _finish_kernel,
        out_shape=(jax.ShapeDtypeStruct((BATCH, SEQ, D_MODEL), jnp.float32),
                   jax.ShapeDtypeStruct((BATCH, SEQ, D_MODEL), jnp.bfloat16),
                   jax.ShapeDtypeStruct((2, N_TOK), jnp.int32),
                   jax.ShapeDtypeStruct((2, N_TOK), jnp.float32),
                   jax.ShapeDtypeStruct((2, N_TOK), jnp.int32),
                   jax.ShapeDtypeStruct((N_EXPERTS, FIN_TILE), jnp.float32)),
        grid=(BATCH, n_lat),
        in_specs=[lat(D_NA), lat(D_RW), lat(D_RW), lat(D_RW), lat(D_MODEL),
                  pl.BlockSpec((1, 6, D_MODEL), lambda b, j: (b, 0, 0)),
                  full((1, D_RW)), full((1, D_RW)), full((D_MODEL, D_MODEL)), full((1, D_MODEL)),
                  full((D_RW, D_RW)), full((LOGIT_ROWS, D_MODEL)), full((LOGIT_ROWS, 1)),
                  full((FIN_TILE, 2 * FIN_TILE))],
        out_specs=(lat(D_MODEL), lat(D_MODEL), tokl, tokl, tokl, full((N_EXPERTS, FIN_TILE))),
        scratch_shapes=[pltpu.VMEM((N_EXPERTS, FIN_TILE), jnp.float32)],
        compiler_params=pltpu.CompilerParams(dimension_semantics=("arbitrary", "arbitrary"),
                                             vmem_limit_bytes=VMEM_LIMIT),
        name="finish",
    )(na_x, o_rw, bonus, g, x, mod3, ln_g, ln_b, w_out_bf16, norm2_g, bd, wr_t, br, uj)


def _moe_kernel(te_ref, tv_ref, tx_ref, xs_ref, w1_ref, w3_ref, w2_ref, y_ref, w1s, w3s, w2s):
    t = pl.program_id(0)

    @pl.when((t == 0) | (te_ref[t] != te_ref[jnp.maximum(t - 1, 0)]))
    def _():
        w1s[...] = w1_ref[0].astype(jnp.bfloat16)
        w3s[...] = w3_ref[0].astype(jnp.bfloat16)
        w2s[...] = w2_ref[0].astype(jnp.bfloat16)

    @pl.when(tv_ref[t] == 1)
    def _():
        xs = xs_ref[...]
        h1 = jnp.dot(xs, w1s[...], preferred_element_type=jnp.float32)
        h3 = jnp.dot(xs, w3s[...], preferred_element_type=jnp.float32)
        he = (_silu(h1) * h3).astype(jnp.bfloat16)
        y_ref[...] = jnp.dot(he, w2s[...], preferred_element_type=jnp.float32).astype(jnp.bfloat16)

    @pl.when(tv_ref[t] == 0)
    def _():
        y_ref[...] = jnp.zeros_like(y_ref)


def _moe_call(tile_e, tile_valid, tile_src, xs, w1, w3, w2):
    return pl.pallas_call(
        _moe_kernel,
        out_shape=jax.ShapeDtypeStruct((N_SLOTS, D_MODEL), jnp.bfloat16),
        grid_spec=pltpu.PrefetchScalarGridSpec(
            num_scalar_prefetch=3,
            grid=(N_MOE_TILES,),
            in_specs=[pl.BlockSpec((MOE_TILE, D_MODEL), lambda t, te, tv, tx: (tx[t], 0)),
                      pl.BlockSpec((1, D_MODEL, D_EXPERT), lambda t, te, tv, tx: (te[t], 0, 0)),
                      pl.BlockSpec((1, D_MODEL, D_EXPERT), lambda t, te, tv, tx: (te[t], 0, 0)),
                      pl.BlockSpec((1, D_EXPERT, D_MODEL), lambda t, te, tv, tx: (te[t], 0, 0))],
            out_specs=pl.BlockSpec((MOE_TILE, D_MODEL), lambda t, te, tv, tx: (t, 0)),
            scratch_shapes=[pltpu.VMEM((D_MODEL, D_EXPERT), jnp.bfloat16),
                            pltpu.VMEM((D_MODEL, D_EXPERT), jnp.bfloat16),
                            pltpu.VMEM((D_EXPERT, D_MODEL), jnp.bfloat16)]),
        compiler_params=pltpu.CompilerParams(dimension_semantics=("arbitrary",), vmem_limit_bytes=VMEM_LIMIT),
        name="moe",
    )(tile_e, tile_valid, tile_src, xs, w1, w3, w2)


def _moe_plan(eid, rank, cnt):
    counts = cnt[:, 0].astype(jnp.int32)
    gsz = ((counts + MOE_TILE - 1) // MOE_TILE) * MOE_TILE
    gend = jnp.cumsum(gsz)
    goff = gend - gsz
    pos = rank
    for e in range(N_EXPERTS):
        pos = pos + jnp.where(eid == e, goff[e], 0)
    starts = jnp.arange(N_MOE_TILES, dtype=jnp.int32) * MOE_TILE
    te = jnp.sum((gend[None, :] <= starts[:, None]).astype(jnp.int32), axis=1)
    valid = (te < N_EXPERTS).astype(jnp.int32)
    last = jnp.max(jnp.where(counts > 0, jnp.arange(N_EXPERTS), 0)).astype(jnp.int32)
    te = jnp.where(valid == 1, te, last)
    n_valid = gend[-1] // MOE_TILE
    tx = jnp.clip(jnp.arange(N_MOE_TILES, dtype=jnp.int32), 0, jnp.maximum(n_valid - 1, 0))
    tok = jnp.broadcast_to(jnp.arange(N_TOK, dtype=jnp.int32), (2, N_TOK))
    src_tok = jnp.zeros((N_SLOTS,), jnp.int32).at[pos.reshape(-1)].set(tok.reshape(-1))
    return pos, src_tok, te, valid, tx


def _final_kernel(x1_ref, ya_ref, yb_ref, wt_ref, mod_ref, o_ref):
    w = wt_ref[...]
    ya = ya_ref[0].astype(jnp.float32)
    yb = yb_ref[0].astype(jnp.float32)
    o_ref[0] = x1_ref[0] + mod_ref[0][5:6, :] * (w[:, 0:1] * ya + w[:, 1:2] * yb)


def _final_call(x1, ya, yb, wt_rows, mod3):
    n_lat = SEQ // TOK_TILE
    lat = pl.BlockSpec((1, TOK_TILE, D_MODEL), lambda b, j: (b, j, 0))
    return pl.pallas_call(
        _final_kernel,
        out_shape=jax.ShapeDtypeStruct((BATCH, SEQ, D_MODEL), jnp.float32),
        grid=(BATCH, n_lat),
        in_specs=[lat, lat, lat, pl.BlockSpec((TOK_TILE, 2), lambda b, j: (b * n_lat + j, 0)),
                  pl.BlockSpec((1, 6, D_MODEL), lambda b, j: (b, 0, 0))],
        out_specs=lat,
        name="final",
    )(x1, ya, yb, wt_rows, mod3)


def _block_diag2(w):
    z = jnp.zeros_like(w[0])
    return jnp.concatenate([jnp.concatenate([w[0], z], axis=1), jnp.concatenate([z, w[1]], axis=1)], axis=0)


def kernel(x, c, ctx, c_ctx, w_mod, b_mod, norm1_g, norm2_g, w_in, na_q_g, na_k_g, na_rpb, rw_mu_prev, rw_mu_next,
           rw_w0, rw_w_up, rw_a0, rw_a_up, rw_g_up, rw_k_k, rw_k_a, rw_r_k, rw_ln_g, rw_ln_b, w_out, moe_wg, moe_bg,
           moe_we, moe_be, moe_w1, moe_w3, moe_w2):
    bf = jnp.bfloat16
    mod_rows = BATCH + 8
    cs = jnp.concatenate([c, c_ctx[None, :], jnp.zeros((mod_rows - BATCH - 1, D_MODEL), jnp.float32)], axis=0)
    mod = _mod_call(cs, w_mod[0], b_mod[0][None, :])
    mod3 = mod.reshape(mod_rows, 6, D_MODEL)

    head = jnp.arange(D_RW) // HEAD_DIM
    bd = (head[:, None] == head[None, :]).astype(bf)
    qkv, p_rw = _inproj_call(x, ctx, mod3, norm1_g[0][None, :], w_in[0].astype(bf),
                             jnp.tile(na_q_g[0], H_NA)[None, :], jnp.tile(na_k_g[0], H_NA)[None, :], bd)

    na_x = _na_call(qkv, _na_bias_table(na_rpb[0]))

    cos_t, sin_t = _rope_tables()
    p_rw_t = p_rw.transpose(1, 0, 2).reshape(T_ALL * BATCH, D_RW_IN)
    v, abar, rbar, ktil, btil, gend, g_t, bonus_t = _prep_call(
        p_rw_t, cos_t, sin_t, rw_mu_prev[0][None, :], rw_mu_next[0][None, :],
        rw_w0[0].reshape(1, 2 * D_RW), _block_diag2(rw_w_up[0]).astype(bf),
        rw_a0[0].reshape(1, 2 * D_RW), _block_diag2(rw_a_up[0]).astype(bf),
        rw_g_up[0].astype(bf), rw_k_k[0][None, :], rw_k_a[0][None, :], rw_r_k[0].reshape(1, D_RW), bd)
    g = g_t.reshape(T_ALL, BATCH, D_RW)[CTX_LEN:].transpose(1, 0, 2)
    bonus = bonus_t.reshape(T_ALL, BATCH, D_RW)[CTX_LEN:].transpose(1, 0, 2)

    o2 = _scan_call(v, abar, rbar, ktil, btil, gend)
    o_rw = _from_scan_layout(o2[0] + o2[1])

    wr_t = jnp.zeros((LOGIT_ROWS, D_MODEL), jnp.float32)
    wr_t = wr_t.at[0:MOE_GROUPS].set(moe_wg[0].T)
    wr_t = wr_t.at[8:].set(moe_we[0].transpose(0, 2, 1).reshape(N_EXPERTS, D_MODEL))
    br = jnp.zeros((LOGIT_ROWS,), jnp.float32).at[0:MOE_GROUPS].set(moe_bg[0]).at[8:].set(moe_be[0].reshape(-1))
    tri = jnp.arange(FIN_TILE)
    uj = jnp.concatenate([(tri[:, None] < tri[None, :]).astype(bf), jnp.ones((FIN_TILE, FIN_TILE), bf)], axis=1)
    x1, h2p, eid, wts, rank, cnt = _finish_call(na_x, o_rw, bonus, g, x, mod3, rw_ln_g[0][None, :],
                                                rw_ln_b[0][None, :], w_out[0].astype(bf), norm2_g[0][None, :], bd,
                                                wr_t, br[:, None], uj)

    pos, src_tok, tile_e, tile_valid, tile_src = _moe_plan(eid, rank, cnt)
    xs = jnp.take(h2p.reshape(N_TOK, D_MODEL), src_tok, axis=0)
    ys = _moe_call(tile_e, tile_valid, tile_src, xs, moe_w1[0], moe_w3[0], moe_w2[0])
    ya = jnp.take(ys, pos[0], axis=0).reshape(BATCH, SEQ, D_MODEL)
    yb = jnp.take(ys, pos[1], axis=0).reshape(BATCH, SEQ, D_MODEL)
    return _final_call(x1, ya, yb, wts.T, mod3)
```

```python
import functools
import math

import jax
import jax.numpy as jnp
import numpy as np
from jax import lax
from jax.experimental import pallas as pl
from jax.experimental.pallas import tpu as pltpu

D_MODEL = 1024
BATCH = 16
SEQ = 2048
GRID_W = 64
GRID_H = SEQ // GRID_W
CTX_LEN = 256
T_ALL = CTX_LEN + SEQ
HEAD_DIM = 64
D_NA = 512
D_RW = 512
H_NA = D_NA // HEAD_DIM
H_RW = D_RW // HEAD_DIM
NA_KH = 8
NA_KW = 16
LORA_W = 64
LORA_A = 64
LORA_G = 128
D_RW_IN = 3 * D_RW + 2 * (LORA_W + LORA_A) + LORA_G
D_IN = 3 * D_NA + D_RW_IN
MOE_GROUPS = 4
MOE_PER_GROUP = 8
N_EXPERTS = MOE_GROUPS * MOE_PER_GROUP
D_EXPERT = 512
ROPE_THETA = 10000.0
NORM_EPS = 1e-6
RW_LN_EPS = 64e-5
NEG = -1e30
DECAY_SCALE = math.exp(-0.5)

LANES = 128
TOK_TILE = 256
FIN_TILE = 512
N_TILES = T_ALL // TOK_TILE
SCAN_CHUNK = 16
MOE_TILE = 256
N_TOK = BATCH * SEQ
N_SLOTS = 2 * N_TOK + N_EXPERTS * MOE_TILE
N_MOE_TILES = N_SLOTS // MOE_TILE
LOGIT_ROWS = 8 + N_EXPERTS
VMEM_LIMIT = 48 * 1024 * 1024

_HI = lax.Precision.HIGHEST


def _sigmoid(x):
    return 1.0 / (1.0 + jnp.exp(-x))


def _silu(x):
    return x * _sigmoid(x)


def _rmsnorm_rows(xf, g):
    return xf * lax.rsqrt(jnp.mean(xf * xf, axis=-1, keepdims=True) + NORM_EPS) * g


def _group_sum(x, bd):
    hi = x.astype(jnp.bfloat16)
    lo = (x - hi.astype(jnp.float32)).astype(jnp.bfloat16)
    return (jnp.dot(hi, bd, preferred_element_type=jnp.float32)
            + jnp.dot(lo, bd, preferred_element_type=jnp.float32))


def _mod_kernel(c_ref, w_ref, b_ref, o_ref):
    o_ref[...] = jnp.dot(_silu(c_ref[...]), w_ref[...], precision=_HI,
                         preferred_element_type=jnp.float32) + b_ref[...]


def _mod_call(cs, w_mod, b_mod):
    rows = cs.shape[0]
    n = w_mod.shape[1]
    blk = 1024
    return pl.pallas_call(
        _mod_kernel,
        out_shape=jax.ShapeDtypeStruct((rows, n), jnp.float32),
        grid=(n // blk,),
        in_specs=[pl.BlockSpec((rows, D_MODEL), lambda j: (0, 0)),
                  pl.BlockSpec((D_MODEL, blk), lambda j: (0, j)),
                  pl.BlockSpec((1, blk), lambda j: (0, j))],
        out_specs=pl.BlockSpec((rows, blk), lambda j: (0, j)),
        name="mod",
    )(cs, w_mod, b_mod)


def _inproj_kernel(x_ref, ctx_ref, modx_ref, modc_ref, g_ref, w_ref, qg_ref, kg_ref, bd_ref, na_ref, rw_ref):
    is_ctx = pl.program_id(1) == 0
    xin = jnp.where(is_ctx, ctx_ref[0], x_ref[0])
    mod = jnp.where(is_ctx, modc_ref[0], modx_ref[0])
    h = _rmsnorm_rows(xin, g_ref[...]) * (1.0 + mod[1:2, :]) + mod[0:1, :]
    p = jnp.dot(h.astype(jnp.bfloat16), w_ref[...], preferred_element_type=jnp.float32)
    rw_ref[0] = p[:, 3 * D_NA:]
    bd = bd_ref[...]
    q = p[:, 0:D_NA]
    k = p[:, D_NA:2 * D_NA]
    qn = q * lax.rsqrt(_group_sum(q * q, bd) * (1.0 / HEAD_DIM) + NORM_EPS) * (qg_ref[...] * HEAD_DIM ** -0.5)
    kn = k * lax.rsqrt(_group_sum(k * k, bd) * (1.0 / HEAD_DIM) + NORM_EPS) * kg_ref[...]
    na_ref[0, :, 0:D_NA] = qn.astype(jnp.bfloat16)
    na_ref[0, :, D_NA:2 * D_NA] = kn.astype(jnp.bfloat16)
    na_ref[0, :, 2 * D_NA:] = p[:, 2 * D_NA:3 * D_NA].astype(jnp.bfloat16)


def _inproj_call(x, ctx, mod3, norm_g, w_in_bf16, q_g, k_g, bd):
    return pl.pallas_call(
        _inproj_kernel,
        out_shape=(jax.ShapeDtypeStruct((BATCH, T_ALL, 3 * D_NA), jnp.bfloat16),
                   jax.ShapeDtypeStruct((BATCH, T_ALL, D_RW_IN), jnp.float32)),
        grid=(BATCH, N_TILES),
        in_specs=[pl.BlockSpec((1, TOK_TILE, D_MODEL), lambda b, j: (b, jnp.maximum(j - 1, 0), 0)),
                  pl.BlockSpec((1, TOK_TILE, D_MODEL), lambda b, j: (b, 0, 0)),
                  pl.BlockSpec((1, 6, D_MODEL), lambda b, j: (b, 0, 0)),
                  pl.BlockSpec((1, 6, D_MODEL), lambda b, j: (BATCH, 0, 0)),
                  pl.BlockSpec((1, D_MODEL), lambda b, j: (0, 0)),
                  pl.BlockSpec((D_MODEL, D_IN), lambda b, j: (0, 0)),
                  pl.BlockSpec((1, D_NA), lambda b, j: (0, 0)),
                  pl.BlockSpec((1, D_NA), lambda b, j: (0, 0)),
                  pl.BlockSpec((D_NA, D_NA), lambda b, j: (0, 0))],
        out_specs=(pl.BlockSpec((1, TOK_TILE, 3 * D_NA), lambda b, j: (b, j, 0)),
                   pl.BlockSpec((1, TOK_TILE, D_RW_IN), lambda b, j: (b, j, 0))),
        compiler_params=pltpu.CompilerParams(vmem_limit_bytes=VMEM_LIMIT),
        name="inproj",
    )(x, ctx, mod3, mod3, norm_g, w_in_bf16, q_g, k_g, bd)


NA_QROWS = 4
NA_QBLK = NA_QROWS * GRID_W
NA_BAND = NA_KH + NA_QROWS - 1
NA_BAND_KEYS = NA_BAND * GRID_W
NA_NBLK = GRID_H // NA_QROWS


def _na_band_start(i):
    return np.clip(i * NA_QROWS - NA_KH // 2, 0, GRID_H - NA_BAND)


def _na_kernel(q_ref, k_ref, v_ref, tz_ref, o_ref, bias_ref):
    @pl.when(pl.program_id(1) == 0)
    def _():
        masked = jnp.full((GRID_W, GRID_W), NEG, jnp.float32)
        for h in range(2):
            for p, i in enumerate((0, 1, NA_NBLK - 1)):
                bs = int(_na_band_start(i))
                for g in range(NA_QROWS):
                    qr = i * NA_QROWS + g
                    rs = int(np.clip(qr - NA_KH // 2, 0, GRID_H - NA_KH))
                    for j in range(NA_BAND):
                        kr = bs + j
                        blk = tz_ref[h, kr - qr + NA_KH - 1] if rs <= kr < rs + NA_KH else masked
                        bias_ref[h, p, g * GRID_W:(g + 1) * GRID_W, j * GRID_W:(j + 1) * GRID_W] = blk

    nt = (((1,), (1,)), ((), ()))
    head0 = lax.broadcasted_iota(jnp.int32, (NA_QBLK, LANES), 1) < HEAD_DIM

    def body(i, carry):
        bs = jnp.clip(i * NA_QROWS - NA_KH // 2, 0, GRID_H - NA_BAND)
        pattern = jnp.where(i == 0, 0, jnp.where(i == NA_NBLK - 1, 2, 1))
        q0 = pl.multiple_of(i * NA_QBLK, NA_QBLK)
        k0 = pl.multiple_of(CTX_LEN + bs * GRID_W, GRID_W)
        q2 = q_ref[0, pl.ds(CTX_LEN + q0, NA_QBLK), :]
        kw = k_ref[0, pl.ds(k0, NA_BAND_KEYS), :]
        vw = v_ref[0, pl.ds(k0, NA_BAND_KEYS), :]
        kc = k_ref[0, 0:CTX_LEN, :]
        vc = v_ref[0, 0:CTX_LEN, :]
        outs = []
        for h in range(2):
            qb = jnp.where(head0 if h == 0 else ~head0, q2, jnp.zeros_like(q2))
            s_win = lax.dot_general(qb, kw, nt, preferred_element_type=jnp.float32) + bias_ref[h, pattern]
            s_ctx = lax.dot_general(qb, kc, nt, preferred_element_type=jnp.float32)
            m = jnp.maximum(jnp.max(s_win, axis=-1, keepdims=True), jnp.max(s_ctx, axis=-1, keepdims=True))
            e_win = jnp.exp(s_win - m)
            e_ctx = jnp.exp(s_ctx - m)
            den = jnp.sum(e_win, axis=-1, keepdims=True) + jnp.sum(e_ctx, axis=-1, keepdims=True)
            o = (jnp.dot(e_win.astype(jnp.bfloat16), vw, preferred_element_type=jnp.float32)
                 + jnp.dot(e_ctx.astype(jnp.bfloat16), vc, preferred_element_type=jnp.float32))
            outs.append(o / den)
        o_ref[0, pl.ds(q0, NA_QBLK), :] = jnp.where(head0, outs[0], outs[1])
        return carry

    lax.fori_loop(0, NA_NBLK, body, 0)


def _na_call(qkv, bias8):
    n_hp = D_NA // LANES
    return pl.pallas_call(
        _na_kernel,
        out_shape=jax.ShapeDtypeStruct((BATCH, SEQ, D_NA), jnp.float32),
        grid=(n_hp, BATCH),
        in_specs=[pl.BlockSpec((1, T_ALL, LANES), lambda hp, b: (b, 0, hp)),
                  pl.BlockSpec((1, T_ALL, LANES), lambda hp, b: (b, 0, n_hp + hp)),
                  pl.BlockSpec((1, T_ALL, LANES), lambda hp, b: (b, 0, 2 * n_hp + hp)),
                  pl.BlockSpec((2, 2 * NA_KH - 1, GRID_W, GRID_W), lambda hp, b: (hp, 0, 0, 0))],
        out_specs=pl.BlockSpec((1, SEQ, LANES), lambda hp, b: (b, 0, hp)),
        scratch_shapes=[pltpu.VMEM((2, 3, NA_QBLK, NA_BAND_KEYS), jnp.float32)],
        compiler_params=pltpu.CompilerParams(dimension_semantics=("arbitrary", "arbitrary"),
                                             vmem_limit_bytes=VMEM_LIMIT),
        name="na",
    )(qkv, qkv, qkv, bias8)


def _na_bias_table(rpb):
    qc = np.arange(GRID_W)
    cs = np.clip(qc - NA_KW // 2, 0, GRID_W - NA_KW)
    kc = np.arange(GRID_W)
    col_ok = (kc[None, :] >= cs[:, None]) & (kc[None, :] < cs[:, None] + NA_KW)
    dc = np.clip(kc[None, :] - qc[:, None], -(NA_KW - 1), NA_KW - 1) + NA_KW - 1
    return jnp.where(col_ok[None, None], rpb[:, :, dc], NEG).astype(jnp.float32)


def _swap16(x):
    lane = lax.broadcasted_iota(jnp.int32, x.shape, 1)
    return jnp.where((lane & 16) == 0, pltpu.roll(x, LANES - 16, axis=1), pltpu.roll(x, 16, axis=1))


def _rope(x, cos, sin):
    blocks = []
    for i in range(x.shape[1] // LANES):
        xb = x[:, i * LANES:(i + 1) * LANES]
        blocks.append(xb * cos + _swap16(xb) * sin)
    return jnp.concatenate(blocks, axis=1)


def _shift_rows(x, n, fill, down):
    if down:
        return jnp.concatenate([fill, x[:-n]], axis=0)
    return jnp.concatenate([x[n:], fill], axis=0)


def _chunk_cumprod(w, reverse):
    inc = w
    s = 1
    while s < SCAN_CHUNK:
        ones = jnp.ones((s * BATCH, w.shape[1]), jnp.float32)
        inc = inc * _shift_rows(inc, s * BATCH, ones, down=not reverse)
        s *= 2
    exc = _shift_rows(inc, BATCH, jnp.ones((BATCH, w.shape[1]), jnp.float32), down=not reverse)
    return inc, exc


def _scan_tiles(u):
    low = lax.broadcasted_iota(jnp.int32, (BATCH, LANES), 1) < HEAD_DIM
    steps = u.shape[0] // BATCH
    tiles = []
    for m in range(0, steps, 2):
        a = u[m * BATCH:(m + 1) * BATCH]
        b = u[(m + 1) * BATCH:(m + 2) * BATCH] if m + 1 < steps else a
        rows = []
        for hp in range(D_RW // LANES):
            pa = a[:, hp * LANES:(hp + 1) * LANES]
            pb = b[:, hp * LANES:(hp + 1) * LANES]
            rows.append(jnp.where(low, pa, pltpu.roll(pb, HEAD_DIM, axis=1)))
            rows.append(jnp.where(low, pltpu.roll(pa, HEAD_DIM, axis=1), pb))
        wt = jnp.concatenate(rows, axis=0).T
        tiles.append(wt[:HEAD_DIM])
        if m + 1 < steps:
            tiles.append(wt[HEAD_DIM:])
    return tiles


def _prep_kernel(p_ref, pv_ref, nx_ref, cos_ref, sin_ref, mup_ref, mun_ref, w0_ref, wup_ref, a0_ref, aup_ref,
                 gup_ref, kk_ref, ka_ref, rk_ref, bd_ref,
                 v_out, abar_out, rbar_out, ktil_out, btil_out, gend_out, g_out, bonus_out, rows_ref):
    j = pl.program_id(0)
    n_ctx = CTX_LEN // SCAN_CHUNK
    p = p_ref[...]
    seq_start = (j == 0) | (j == n_ctx)
    seq_end = (j == n_ctx - 1) | (j == pl.num_programs(0) - 1)
    prev = _shift_rows(p, BATCH, jnp.where(seq_start, 0.0, pv_ref[...]), down=True)
    nxt = _shift_rows(p, BATCH, jnp.where(seq_end, 0.0, nx_ref[...]), down=False)
    ps = p + mup_ref[...] * (prev - p) + mun_ref[...] * (nxt - p)

    cos = cos_ref[...]
    sin = sin_ref[...]
    r = _rope(ps[:, 0:D_RW], cos, sin)
    k = _rope(ps[:, D_RW:2 * D_RW], cos, sin)
    v = ps[:, 2 * D_RW:3 * D_RW]
    o3 = 3 * D_RW
    wd = ps[:, o3:o3 + 2 * LORA_W]
    ad = ps[:, o3 + 2 * LORA_W:o3 + 2 * LORA_W + 2 * LORA_A]
    gd = ps[:, o3 + 2 * LORA_W + 2 * LORA_A:]

    w_pre = w0_ref[...] + jnp.dot(jnp.tanh(wd).astype(jnp.bfloat16), wup_ref[...], preferred_element_type=jnp.float32)
    decay = jnp.exp(-DECAY_SCALE * _sigmoid(w_pre))
    a = _sigmoid(a0_ref[...] + jnp.dot(ad.astype(jnp.bfloat16), aup_ref[...], preferred_element_type=jnp.float32))
    g = jnp.dot(_sigmoid(gd).astype(jnp.bfloat16), gup_ref[...], preferred_element_type=jnp.float32)

    bd = bd_ref[...]
    kk = k * kk_ref[...]
    kk = kk / jnp.maximum(jnp.sqrt(_group_sum(kk * kk, bd)), 1e-12)
    bonus = _group_sum(r * k * rk_ref[...], bd) * v
    ka = ka_ref[...]

    for out, val in ((g_out, g), (bonus_out, bonus)):
        for cb in range(D_RW // LANES):
            rows_ref[cb] = val[:, cb * LANES:(cb + 1) * LANES]
        for b in range(BATCH):
            for cb in range(D_RW // LANES):
                out[b, :, cb * LANES:(cb + 1) * LANES] = rows_ref[cb, pl.ds(b, SCAN_CHUNK, stride=BATCH), :]

    for t, tile in enumerate(_scan_tiles(v)):
        v_out[t] = tile
    for d in range(2):
        ds_ = slice(d * D_RW, (d + 1) * D_RW)
        a_d = a[:, ds_]
        gam, gam_excl = _chunk_cumprod(decay[:, ds_], reverse=(d == 1))
        inv = 1.0 / gam
        streams = ((abar_out, gam_excl * kk), (rbar_out, gam * r),
                   (ktil_out, k * (1.0 + (a_d - 1.0) * ka) * inv), (btil_out, kk * a_d * inv))
        for out, val in streams:
            for t, tile in enumerate(_scan_tiles(val)):
                out[d, t] = tile
        gam_last = gam[:BATCH] if d == 1 else gam[(SCAN_CHUNK - 1) * BATCH:]
        gend_out[d, 0] = _scan_tiles(gam_last)[0]


def _prep_call(p_rw_t, cos_t, sin_t, mu_prev, mu_next, w0, wup2, a0, aup2, gup, k_k, k_a, r_k, bd):
    rows = SCAN_CHUNK * BATCH
    n_chunks = T_ALL // SCAN_CHUNK
    full = lambda shape: pl.BlockSpec(shape, lambda j: (0,) * len(shape))
    tok = lambda width: pl.BlockSpec((rows, width), lambda j: (j, 0))
    perdir = pl.BlockSpec((2, SCAN_CHUNK, HEAD_DIM, LANES), lambda j: (0, j, 0, 0))
    two = jax.ShapeDtypeStruct((2, T_ALL, HEAD_DIM, LANES), jnp.float32)
    nat = jax.ShapeDtypeStruct((BATCH, SEQ, D_RW), jnp.float32)
    n_ctx = CTX_LEN // SCAN_CHUNK
    lat = pl.BlockSpec((BATCH, SCAN_CHUNK, D_RW), lambda j: (0, jnp.maximum(j - n_ctx, 0), 0))
    return pl.pallas_call(
        _prep_kernel,
        out_shape=(jax.ShapeDtypeStruct((T_ALL, HEAD_DIM, LANES), jnp.float32), two, two, two, two,
                   jax.ShapeDtypeStruct((2, n_chunks, HEAD_DIM, LANES), jnp.float32), nat, nat),
        grid=(n_chunks,),
        in_specs=[tok(D_RW_IN),
                  pl.BlockSpec((BATCH, D_RW_IN), lambda j: (jnp.maximum(j * SCAN_CHUNK - 1, 0), 0)),
                  pl.BlockSpec((BATCH, D_RW_IN), lambda j: (jnp.minimum((j + 1) * SCAN_CHUNK, T_ALL - 1), 0)),
                  tok(LANES), tok(LANES),
                  full((1, D_RW_IN)), full((1, D_RW_IN)),
                  full((1, 2 * D_RW)), full((2 * LORA_W, 2 * D_RW)),
                  full((1, 2 * D_RW)), full((2 * LORA_A, 2 * D_RW)),
                  full((LORA_G, D_RW)),
                  full((1, D_RW)), full((1, D_RW)), full((1, D_RW)),
                  full((D_RW, D_RW))],
        out_specs=(pl.BlockSpec((SCAN_CHUNK, HEAD_DIM, LANES), lambda j: (j, 0, 0)), perdir, perdir, perdir, perdir,
                   pl.BlockSpec((2, 1, HEAD_DIM, LANES), lambda j: (0, j, 0, 0)), lat, lat),
        scratch_shapes=[pltpu.VMEM((D_RW // LANES, SCAN_CHUNK * BATCH, LANES), jnp.float32)],
        compiler_params=pltpu.CompilerParams(dimension_semantics=("arbitrary",), vmem_limit_bytes=VMEM_LIMIT),
        name="prep",
    )(p_rw_t, p_rw_t, p_rw_t, cos_t, sin_t, mu_prev, mu_next, w0, wup2, a0, aup2, gup, k_k, k_a, r_k, bd)


def _rope_tables():
    nf = HEAD_DIM // 4
    pos = np.arange(SEQ)
    inv = ROPE_THETA ** (-np.arange(nf, dtype=np.float32) / nf)
    lane = np.arange(LANES) % HEAD_DIM
    half, pair, f = lane // 32, (lane % 32) // 16, lane % 16
    coord = np.where(half[None, :] == 0, (pos // GRID_W)[:, None], (pos % GRID_W)[:, None]).astype(np.float32)
    ang = coord * inv[f][None, :].astype(np.float32)
    cos = np.cos(ang).astype(np.float32)
    sin = np.sin(ang).astype(np.float32) * np.where(pair == 0, -1.0, 1.0)[None, :].astype(np.float32)
    cos = np.concatenate([np.ones((CTX_LEN, LANES), np.float32), cos], axis=0)
    sin = np.concatenate([np.zeros((CTX_LEN, LANES), np.float32), sin], axis=0)
    return jnp.asarray(np.repeat(cos, BATCH, axis=0)), jnp.asarray(np.repeat(sin, BATCH, axis=0))


def _scan_kernel(v_ref, abar_ref, rbar_ref, ktil_ref, btil_ref, gend_ref, o_ref, s_ref):
    d = pl.program_id(0)
    i = pl.program_id(1)

    @pl.when(i == 0)
    def _():
        s_ref[...] = jnp.zeros_like(s_ref)

    k_unroll = 32
    n_kb = HEAD_DIM // k_unroll
    zero = jnp.zeros((HEAD_DIM, LANES), jnp.float32)

    def make_step(with_out):
        def step(s, carry):
            t = jnp.where(d == 0, s, SCAN_CHUNK - 1 - s)
            vv = v_ref[t]

            def sk_body(kb, sk):
                for u in range(k_unroll):
                    k = kb * k_unroll + u
                    sk = sk + s_ref[k] * abar_ref[0, t, pl.ds(k, 1), :]
                return sk

            sk = lax.fori_loop(0, n_kb, sk_body, zero)

            def upd_body(kb, o):
                for u in range(k_unroll):
                    k = kb * k_unroll + u
                    s_new = s_ref[k] + (vv * ktil_ref[0, t, pl.ds(k, 1), :] - sk * btil_ref[0, t, pl.ds(k, 1), :])
                    s_ref[k] = s_new
                    if with_out:
                        o = o + s_new * rbar_ref[0, t, pl.ds(k, 1), :]
                return o

            o = lax.fori_loop(0, n_kb, upd_body, zero)
            if with_out:
                o_ref[0, t] = o
            return carry
        return step

    @pl.when(i < CTX_LEN // SCAN_CHUNK)
    def _():
        lax.fori_loop(0, SCAN_CHUNK, make_step(False), 0)

    @pl.when(i >= CTX_LEN // SCAN_CHUNK)
    def _():
        lax.fori_loop(0, SCAN_CHUNK, make_step(True), 0)

    def renorm(kb, carry):
        for u in range(k_unroll):
            k = kb * k_unroll + u
            s_ref[k] = s_ref[k] * gend_ref[0, 0, pl.ds(k, 1), :]
        return carry

    lax.fori_loop(0, n_kb, renorm, 0)


def _scan_call(v_t, abar_t, rbar_t, ktil_t, btil_t, gend_t):
    nc = CTX_LEN // SCAN_CHUNK
    nl = SEQ // SCAN_CHUNK

    def blk(d, i):
        rev = jnp.where(i < nc, nc - 1 - i, 2 * nc + nl - 1 - i)
        return jnp.where(d == 0, i, rev)

    def oblk(d, i):
        fwd = jnp.maximum(i - nc, 0)
        rev = jnp.where(i < nc, nl - 1, nc + nl - 1 - i)
        return jnp.where(d == 0, fwd, rev)

    shared = pl.BlockSpec((SCAN_CHUNK, HEAD_DIM, LANES), lambda d, i: (blk(d, i), 0, 0))
    perdir = pl.BlockSpec((1, SCAN_CHUNK, HEAD_DIM, LANES), lambda d, i: (d, blk(d, i), 0, 0))
    perchunk = pl.BlockSpec((1, 1, HEAD_DIM, LANES), lambda d, i: (d, blk(d, i), 0, 0))
    return pl.pallas_call(
        _scan_kernel,
        out_shape=jax.ShapeDtypeStruct((2, SEQ, HEAD_DIM, LANES), jnp.float32),
        grid=(2, nc + nl),
        in_specs=[shared, perdir, perdir, perdir, perdir, perchunk],
        out_specs=pl.BlockSpec((1, SCAN_CHUNK, HEAD_DIM, LANES), lambda d, i: (d, oblk(d, i), 0, 0)),
        scratch_shapes=[pltpu.VMEM((HEAD_DIM, HEAD_DIM, LANES), jnp.float32)],
        compiler_params=pltpu.CompilerParams(dimension_semantics=("arbitrary", "arbitrary"),
                                             vmem_limit_bytes=VMEM_LIMIT),
        name="scan",
    )(v_t, abar_t, rbar_t, ktil_t, btil_t, gend_t)


def _from_scan_layout(o):
    return o.reshape(SEQ, HEAD_DIM, H_RW, BATCH).transpose(3, 0, 2, 1).reshape(BATCH, SEQ, D_RW)


def _finish_kernel(na_ref, o_ref, bonus_ref, g_ref, x_ref, mod_ref, lng_ref, lnb_ref, wout_ref, n2g_ref, bd_ref,
                   wr_ref, br_ref, uj_ref, x1_out, h2_out, eid_out, wt_out, rank_out, cnt_out, run_ref):
    @pl.when((pl.program_id(0) == 0) & (pl.program_id(1) == 0))
    def _():
        run_ref[...] = jnp.zeros_like(run_ref)

    bd = bd_ref[...]
    o = o_ref[0]
    mu = _group_sum(o, bd) * (1.0 / HEAD_DIM)
    oc = o - mu
    var = _group_sum(oc * oc, bd) * (1.0 / HEAD_DIM)
    y = oc * lax.rsqrt(var + RW_LN_EPS) * lng_ref[...] + lnb_ref[...]
    rw = (y + bonus_ref[0]) * g_ref[0]
    mix = jnp.concatenate([na_ref[0], rw], axis=-1).astype(jnp.bfloat16)
    yx = jnp.dot(mix, wout_ref[...], preferred_element_type=jnp.float32)
    mod = mod_ref[0]
    x1 = x_ref[0] + mod[2:3, :] * yx
    x1_out[0] = x1
    h2 = _rmsnorm_rows(x1, n2g_ref[...]) * (1.0 + mod[4:5, :]) + mod[3:4, :]
    h2_out[0] = h2.astype(jnp.bfloat16)

    lg = lax.dot_general(wr_ref[...], h2, (((1,), (1,)), ((), ())), precision=_HI,
                         preferred_element_type=jnp.float32) + br_ref[...]
    gl = lg[0:MOE_GROUPS, :]
    grow = lax.broadcasted_iota(jnp.int32, gl.shape, 0)
    gmax = jnp.max(gl, axis=0, keepdims=True)
    g_sel = jnp.min(jnp.where(gl == gmax, grow, MOE_GROUPS), axis=0, keepdims=True)
    p_group = 1.0 / jnp.sum(jnp.exp(gl - gmax), axis=0, keepdims=True)
    el = jnp.zeros((MOE_PER_GROUP, FIN_TILE), jnp.float32)
    for gi in range(MOE_GROUPS):
        el = jnp.where(g_sel == gi, lg[8 + gi * MOE_PER_GROUP:8 + (gi + 1) * MOE_PER_GROUP, :], el)
    erow = lax.broadcasted_iota(jnp.int32, el.shape, 0)
    m1 = jnp.max(el, axis=0, keepdims=True)
    i1 = jnp.min(jnp.where(el == m1, erow, MOE_PER_GROUP), axis=0, keepdims=True)
    el2 = jnp.where(erow == i1, -jnp.inf, el)
    m2 = jnp.max(el2, axis=0, keepdims=True)
    i2 = jnp.min(jnp.where(el2 == m2, erow, MOE_PER_GROUP), axis=0, keepdims=True)
    e21 = jnp.exp(m2 - m1)
    e0 = g_sel * MOE_PER_GROUP + i1
    e1 = g_sel * MOE_PER_GROUP + i2
    eid_out[0:1, :] = e0
    eid_out[1:2, :] = e1
    wt_out[0:1, :] = p_group / (1.0 + e21)
    wt_out[1:2, :] = p_group * e21 / (1.0 + e21)

    xrow = lax.broadcasted_iota(jnp.int32, (N_EXPERTS, FIN_TILE), 0)
    m0 = xrow == e0
    m1b = xrow == e1
    c0 = jnp.dot(jnp.where(m0, 1.0, 0.0).astype(jnp.bfloat16), uj_ref[...], preferred_element_type=jnp.float32)
    c1 = jnp.dot(jnp.where(m1b, 1.0, 0.0).astype(jnp.bfloat16), uj_ref[...], preferred_element_type=jnp.float32)
    run = run_ref[...]
    tot0 = c0[:, FIN_TILE:]
    rank_out[0:1, :] = jnp.sum(jnp.where(m0, run + c0[:, :FIN_TILE], 0.0), axis=0, keepdims=True).astype(jnp.int32)
    rank_out[1:2, :] = jnp.sum(jnp.where(m1b, run + tot0 + c1[:, :FIN_TILE], 0.0), axis=0,
                               keepdims=True).astype(jnp.int32)
    run = run + tot0 + c1[:, FIN_TILE:]
    run_ref[...] = run
    cnt_out[...] = run

def _finish_call(na_x, o_rw, bonus, g, x, mod3, ln_g, ln_b, w_out_bf16, norm2_g, bd, wr_t, br, uj):
    n_lat = SEQ // FIN_TILE
    full = lambda shape: pl.BlockSpec(shape, lambda b, j: (0,) * len(shape))
    lat = lambda width: pl.BlockSpec((1, FIN_TILE, width), lambda b, j: (b, j, 0))
    tokl = pl.BlockSpec((2, FIN_TILE), lambda b, j: (0, b * n_lat + j))
    return pl.pallas_call(
        _finish_kernel,
        out_shape=(jax.ShapeDtypeStruct((BATCH, SEQ, D_MODEL), jnp.float32),
                   jax.ShapeDtypeStruct((BATCH, SEQ, D_MODEL), jnp.bfloat16),
                   jax.ShapeDtypeStruct((2, N_TOK), jnp.int32),
                   jax.ShapeDtypeStruct((2, N_TOK), jnp.float32),
                   jax.ShapeDtypeStruct((2, N_TOK), jnp.int32),
                   jax.ShapeDtypeStruct((N_EXPERTS, FIN_TILE), jnp.float32)),
        grid=(BATCH, n_lat),
        in_specs=[lat(D_NA), lat(D_RW), lat(D_RW), lat(D_RW), lat(D_MODEL),
                  pl.BlockSpec((1, 6, D_MODEL), lambda b, j: (b, 0, 0)),
                  full((1, D_RW)), full((1, D_RW)), full((D_MODEL, D_MODEL)), full((1, D_MODEL)),
                  full((D_RW, D_RW)), full((LOGIT_ROWS, D_MODEL)), full((LOGIT_ROWS, 1)),
                  full((FIN_TILE, 2 * FIN_TILE))],
        out_specs=(lat(D_MODEL), lat(D_MODEL), tokl, tokl, tokl, full((N_EXPERTS, FIN_TILE))),
        scratch_shapes=[pltpu.VMEM((N_EXPERTS, FIN_TILE), jnp.float32)],
        compiler_params=pltpu.CompilerParams(dimension_semantics=("arbitrary", "arbitrary"),
                                             vmem_limit_bytes=VMEM_LIMIT),
        name="finish",
    )(na_x, o_rw, bonus, g, x, mod3, ln_g, ln_b, w_out_bf16, norm2_g, bd, wr_t, br, uj)


def _moe_kernel(te_ref, tv_ref, tx_ref, xs_ref, w1_ref, w3_ref, w2_ref, y_ref, w1s, w3s, w2s):
    t = pl.program_id(0)

    @pl.when((t == 0) | (te_ref[t] != te_ref[jnp.maximum(t - 1, 0)]))
    def _():
        w1s[...] = w1_ref[0].astype(jnp.bfloat16)
        w3s[...] = w3_ref[0].astype(jnp.bfloat16)
        w2s[...] = w2_ref[0].astype(jnp.bfloat16)

    @pl.when(tv_ref[t] == 1)
    def _():
        xs = xs_ref[...]
        h1 = jnp.dot(xs, w1s[...], preferred_element_type=jnp.float32)
        h3 = jnp.dot(xs, w3s[...], preferred_element_type=jnp.float32)
        he = (_silu(h1) * h3).astype(jnp.bfloat16)
        y_ref[...] = jnp.dot(he, w2s[...], preferred_element_type=jnp.float32).astype(jnp.bfloat16)

    @pl.when(tv_ref[t] == 0)
    def _():
        y_ref[...] = jnp.zeros_like(y_ref)


def _moe_call(tile_e, tile_valid, tile_src, xs, w1, w3, w2):
    return pl.pallas_call(
        _moe_kernel,
        out_shape=jax.ShapeDtypeStruct((N_SLOTS, D_MODEL), jnp.bfloat16),
        grid_spec=pltpu.PrefetchScalarGridSpec(
            num_scalar_prefetch=3,
            grid=(N_MOE_TILES,),
            in_specs=[pl.BlockSpec((MOE_TILE, D_MODEL), lambda t, te, tv, tx: (tx[t], 0)),
                      pl.BlockSpec((1, D_MODEL, D_EXPERT), lambda t, te, tv, tx: (te[t], 0, 0)),
                      pl.BlockSpec((1, D_MODEL, D_EXPERT), lambda t, te, tv, tx: (te[t], 0, 0)),
                      pl.BlockSpec((1, D_EXPERT, D_MODEL), lambda t, te, tv, tx: (te[t], 0, 0))],
            out_specs=pl.BlockSpec((MOE_TILE, D_MODEL), lambda t, te, tv, tx: (t, 0)),
            scratch_shapes=[pltpu.VMEM((D_MODEL, D_EXPERT), jnp.bfloat16),
                            pltpu.VMEM((D_MODEL, D_EXPERT), jnp.bfloat16),
                            pltpu.VMEM((D_EXPERT, D_MODEL), jnp.bfloat16)]),
        compiler_params=pltpu.CompilerParams(dimension_semantics=("arbitrary",), vmem_limit_bytes=VMEM_LIMIT),
        name="moe",
    )(tile_e, tile_valid, tile_src, xs, w1, w3, w2)


def _moe_plan(eid, rank, cnt):
    counts = cnt[:, 0].astype(jnp.int32)
    gsz = ((counts + MOE_TILE - 1) // MOE_TILE) * MOE_TILE
    gend = jnp.cumsum(gsz)
    goff = gend - gsz
    pos = rank
    for e in range(N_EXPERTS):
        pos = pos + jnp.where(eid == e, goff[e], 0)
    starts = jnp.arange(N_MOE_TILES, dtype=jnp.int32) * MOE_TILE
    te = jnp.sum((gend[None, :] <= starts[:, None]).astype(jnp.int32), axis=1)
    valid = (te < N_EXPERTS).astype(jnp.int32)
    last = jnp.max(jnp.where(counts > 0, jnp.arange(N_EXPERTS), 0)).astype(jnp.int32)
    te = jnp.where(valid == 1, te, last)
    n_valid = gend[-1] // MOE_TILE
    tx = jnp.clip(jnp.arange(N_MOE_TILES, dtype=jnp.int32), 0, jnp.maximum(n_valid - 1, 0))
    tok = jnp.broadcast_to(jnp.arange(N_TOK, dtype=jnp.int32), (2, N_TOK))
    src_tok = jnp.zeros((N_SLOTS,), jnp.int32).at[pos.reshape(-1)].set(tok.reshape(-1))
    return pos, src_tok, te, valid, tx


def _final_kernel(x1_ref, ya_ref, yb_ref, wt_ref, mod_ref, o_ref):
    w = wt_ref[...]
    ya = ya_ref[0].astype(jnp.float32)
    yb = yb_ref[0].astype(jnp.float32)
    o_ref[0] = x1_ref[0] + mod_ref[0][5:6, :] * (w[:, 0:1] * ya + w[:, 1:2] * yb)


def _final_call(x1, ya, yb, wt_rows, mod3):
    n_lat = SEQ // TOK_TILE
    lat = pl.BlockSpec((1, TOK_TILE, D_MODEL), lambda b, j: (b, j, 0))
    return pl.pallas_call(
        _final_kernel,
        out_shape=jax.ShapeDtypeStruct((BATCH, SEQ, D_MODEL), jnp.float32),
        grid=(BATCH, n_lat),
        in_specs=[lat, lat, lat, pl.BlockSpec((TOK_TILE, 2), lambda b, j: (b * n_lat + j, 0)),
                  pl.BlockSpec((1, 6, D_MODEL), lambda b, j: (b, 0, 0))],
        out_specs=lat,
        name="final",
    )(x1, ya, yb, wt_rows, mod3)


def _block_diag2(w):
    z = jnp.zeros_like(w[0])
    return jnp.concatenate([jnp.concatenate([w[0], z], axis=1), jnp.concatenate([z, w[1]], axis=1)], axis=0)


def kernel(x, c, ctx, c_ctx, w_mod, b_mod, norm1_g, norm2_g, w_in, na_q_g, na_k_g, na_rpb, rw_mu_prev, rw_mu_next,
           rw_w0, rw_w_up, rw_a0, rw_a_up, rw_g_up, rw_k_k, rw_k_a, rw_r_k, rw_ln_g, rw_ln_b, w_out, moe_wg, moe_bg,
           moe_we, moe_be, moe_w1, moe_w3, moe_w2):
    bf = jnp.bfloat16
    mod_rows = BATCH + 8
    cs = jnp.concatenate([c, c_ctx[None, :], jnp.zeros((mod_rows - BATCH - 1, D_MODEL), jnp.float32)], axis=0)
    mod = _mod_call(cs, w_mod[0], b_mod[0][None, :])
    mod3 = mod.reshape(mod_rows, 6, D_MODEL)

    head = jnp.arange(D_RW) // HEAD_DIM
    bd = (head[:, None] == head[None, :]).astype(bf)
    qkv, p_rw = _inproj_call(x, ctx, mod3, norm1_g[0][None, :], w_in[0].astype(bf),
                             jnp.tile(na_q_g[0], H_NA)[None, :], jnp.tile(na_k_g[0], H_NA)[None, :], bd)

    na_x = _na_call(qkv, _na_bias_table(na_rpb[0]))

    cos_t, sin_t = _rope_tables()
    p_rw_t = p_rw.transpose(1, 0, 2).reshape(T_ALL * BATCH, D_RW_IN)
    v, abar, rbar, ktil, btil, gend, g, bonus = _prep_call(
        p_rw_t, cos_t, sin_t, rw_mu_prev[0][None, :], rw_mu_next[0][None, :],
        rw_w0[0].reshape(1, 2 * D_RW), _block_diag2(rw_w_up[0]).astype(bf),
        rw_a0[0].reshape(1, 2 * D_RW), _block_diag2(rw_a_up[0]).astype(bf),
        rw_g_up[0].astype(bf), rw_k_k[0][None, :], rw_k_a[0][None, :], rw_r_k[0].reshape(1, D_RW), bd)

    o2 = _scan_call(v, abar, rbar, ktil, btil, gend)
    o_rw = _from_scan_layout(o2[0] + o2[1])

    wr_t = jnp.zeros((LOGIT_ROWS, D_MODEL), jnp.float32)
    wr_t = wr_t.at[0:MOE_GROUPS].set(moe_wg[0].T)
    wr_t = wr_t.at[8:].set(moe_we[0].transpose(0, 2, 1).reshape(N_EXPERTS, D_MODEL))
    br = jnp.zeros((LOGIT_ROWS,), jnp.float32).at[0:MOE_GROUPS].set(moe_bg[0]).at[8:].set(moe_be[0].reshape(-1))
    tri = jnp.arange(FIN_TILE)
    uj = jnp.concatenate([(tri[:, None] < tri[None, :]).astype(bf), jnp.ones((FIN_TILE, FIN_TILE), bf)], axis=1)
    x1, h2p, eid, wts, rank, cnt = _finish_call(na_x, o_rw, bonus, g, x, mod3, rw_ln_g[0][None, :],
                                                rw_ln_b[0][None, :], w_out[0].astype(bf), norm2_g[0][None, :], bd,
                                                wr_t, br[:, None], uj)

    pos, src_tok, tile_e, tile_valid, tile_src = _moe_plan(eid, rank, cnt)
    xs = jnp.take(h2p.reshape(N_TOK, D_MODEL), src_tok, axis=0, mode="clip")
    ys = _moe_call(tile_e, tile_valid, tile_src, xs, moe_w1[0], moe_w3[0], moe_w2[0])
    ya = jnp.take(ys, pos[0], axis=0, mode="clip").reshape(BATCH, SEQ, D_MODEL)
    yb = jnp.take(ys, pos[1], axis=0, mode="clip").reshape(BATCH, SEQ, D_MODEL)
    return _final_call(x1, ya, yb, wts.T, mod3)
```

```python
import functools
import math

import jax
import jax.numpy as jnp
import numpy as np
from jax import lax
from jax.experimental import pallas as pl
from jax.experimental.pallas import tpu as pltpu

D_MODEL = 1024
BATCH = 16
SEQ = 2048
GRID_W = 64
GRID_H = SEQ // GRID_W
CTX_LEN = 256
T_ALL = CTX_LEN + SEQ
HEAD_DIM = 64
D_NA = 512
D_RW = 512
H_NA = D_NA // HEAD_DIM
H_RW = D_RW // HEAD_DIM
NA_KH = 8
NA_KW = 16
LORA_W = 64
LORA_A = 64
LORA_G = 128
D_RW_IN = 3 * D_RW + 2 * (LORA_W + LORA_A) + LORA_G
D_IN = 3 * D_NA + D_RW_IN
MOE_GROUPS = 4
MOE_PER_GROUP = 8
N_EXPERTS = MOE_GROUPS * MOE_PER_GROUP
D_EXPERT = 512
ROPE_THETA = 10000.0
NORM_EPS = 1e-6
RW_LN_EPS = 64e-5
NEG = -1e30
DECAY_SCALE = math.exp(-0.5)

LANES = 128
TOK_TILE = 256
FIN_TILE = 512
N_TILES = T_ALL // TOK_TILE
SCAN_CHUNK = 16
MOE_TILE = 256
MOE_PARTS = 2
PART_BATCH = BATCH // MOE_PARTS
N_TOK = PART_BATCH * SEQ
N_SLOTS = 2 * N_TOK + N_EXPERTS * MOE_TILE
N_MOE_TILES = N_SLOTS // MOE_TILE
LOGIT_ROWS = 8 + N_EXPERTS
VMEM_LIMIT = 48 * 1024 * 1024

_HI = lax.Precision.HIGHEST


def _sigmoid(x):
    return 1.0 / (1.0 + jnp.exp(-x))


def _silu(x):
    return x * _sigmoid(x)


def _rmsnorm_rows(xf, g):
    return xf * lax.rsqrt(jnp.mean(xf * xf, axis=-1, keepdims=True) + NORM_EPS) * g


def _group_sum(x, bd):
    hi = x.astype(jnp.bfloat16)
    lo = (x - hi.astype(jnp.float32)).astype(jnp.bfloat16)
    return (jnp.dot(hi, bd, preferred_element_type=jnp.float32)
            + jnp.dot(lo, bd, preferred_element_type=jnp.float32))


def _mod_kernel(c_ref, w_ref, b_ref, o_ref):
    o_ref[...] = jnp.dot(_silu(c_ref[...]), w_ref[...], precision=_HI,
                         preferred_element_type=jnp.float32) + b_ref[...]


def _mod_call(cs, w_mod, b_mod):
    rows = cs.shape[0]
    n = w_mod.shape[1]
    blk = 1024
    return pl.pallas_call(
        _mod_kernel,
        out_shape=jax.ShapeDtypeStruct((rows, n), jnp.float32),
        grid=(n // blk,),
        in_specs=[pl.BlockSpec((rows, D_MODEL), lambda j: (0, 0)),
                  pl.BlockSpec((D_MODEL, blk), lambda j: (0, j)),
                  pl.BlockSpec((1, blk), lambda j: (0, j))],
        out_specs=pl.BlockSpec((rows, blk), lambda j: (0, j)),
        name="mod",
    )(cs, w_mod, b_mod)


def _inproj_kernel(x_ref, ctx_ref, modx_ref, modc_ref, g_ref, w_ref, qg_ref, kg_ref, bd_ref, na_ref, rw_ref):
    is_ctx = pl.program_id(1) == 0
    xin = jnp.where(is_ctx, ctx_ref[0], x_ref[0])
    mod = jnp.where(is_ctx, modc_ref[0], modx_ref[0])
    h = _rmsnorm_rows(xin, g_ref[...]) * (1.0 + mod[1:2, :]) + mod[0:1, :]
    p = jnp.dot(h.astype(jnp.bfloat16), w_ref[...], preferred_element_type=jnp.float32)
    rw_ref[0] = p[:, 3 * D_NA:]
    bd = bd_ref[...]
    q = p[:, 0:D_NA]
    k = p[:, D_NA:2 * D_NA]
    qn = q * lax.rsqrt(_group_sum(q * q, bd) * (1.0 / HEAD_DIM) + NORM_EPS) * (qg_ref[...] * HEAD_DIM ** -0.5)
    kn = k * lax.rsqrt(_group_sum(k * k, bd) * (1.0 / HEAD_DIM) + NORM_EPS) * kg_ref[...]
    na_ref[0, :, 0:D_NA] = qn.astype(jnp.bfloat16)
    na_ref[0, :, D_NA:2 * D_NA] = kn.astype(jnp.bfloat16)
    na_ref[0, :, 2 * D_NA:] = p[:, 2 * D_NA:3 * D_NA].astype(jnp.bfloat16)


def _inproj_call(x, ctx, mod3, norm_g, w_in_bf16, q_g, k_g, bd):
    return pl.pallas_call(
        _inproj_kernel,
        out_shape=(jax.ShapeDtypeStruct((BATCH, T_ALL, 3 * D_NA), jnp.bfloat16),
                   jax.ShapeDtypeStruct((BATCH, T_ALL, D_RW_IN), jnp.float32)),
        grid=(BATCH, N_TILES),
        in_specs=[pl.BlockSpec((1, TOK_TILE, D_MODEL), lambda b, j: (b, jnp.maximum(j - 1, 0), 0)),
                  pl.BlockSpec((1, TOK_TILE, D_MODEL), lambda b, j: (b, 0, 0)),
                  pl.BlockSpec((1, 6, D_MODEL), lambda b, j: (b, 0, 0)),
                  pl.BlockSpec((1, 6, D_MODEL), lambda b, j: (BATCH, 0, 0)),
                  pl.BlockSpec((1, D_MODEL), lambda b, j: (0, 0)),
                  pl.BlockSpec((D_MODEL, D_IN), lambda b, j: (0, 0)),
                  pl.BlockSpec((1, D_NA), lambda b, j: (0, 0)),
                  pl.BlockSpec((1, D_NA), lambda b, j: (0, 0)),
                  pl.BlockSpec((D_NA, D_NA), lambda b, j: (0, 0))],
        out_specs=(pl.BlockSpec((1, TOK_TILE, 3 * D_NA), lambda b, j: (b, j, 0)),
                   pl.BlockSpec((1, TOK_TILE, D_RW_IN), lambda b, j: (b, j, 0))),
        compiler_params=pltpu.CompilerParams(vmem_limit_bytes=VMEM_LIMIT),
        name="inproj",
    )(x, ctx, mod3, mod3, norm_g, w_in_bf16, q_g, k_g, bd)


NA_QROWS = 4
NA_QBLK = NA_QROWS * GRID_W
NA_BAND = NA_KH + NA_QROWS - 1
NA_BAND_KEYS = NA_BAND * GRID_W
NA_NBLK = GRID_H // NA_QROWS


def _na_band_start(i):
    return np.clip(i * NA_QROWS - NA_KH // 2, 0, GRID_H - NA_BAND)


def _na_kernel(q_ref, k_ref, v_ref, tz_ref, o_ref, bias_ref):
    @pl.when(pl.program_id(1) == 0)
    def _():
        masked = jnp.full((GRID_W, GRID_W), NEG, jnp.float32)
        for h in range(2):
            for p, i in enumerate((0, 1, NA_NBLK - 1)):
                bs = int(_na_band_start(i))
                for g in range(NA_QROWS):
                    qr = i * NA_QROWS + g
                    rs = int(np.clip(qr - NA_KH // 2, 0, GRID_H - NA_KH))
                    for j in range(NA_BAND):
                        kr = bs + j
                        blk = tz_ref[h, kr - qr + NA_KH - 1] if rs <= kr < rs + NA_KH else masked
                        bias_ref[h, p, g * GRID_W:(g + 1) * GRID_W, j * GRID_W:(j + 1) * GRID_W] = blk

    nt = (((1,), (1,)), ((), ()))
    head0 = lax.broadcasted_iota(jnp.int32, (NA_QBLK, LANES), 1) < HEAD_DIM

    def body(i, carry):
        bs = jnp.clip(i * NA_QROWS - NA_KH // 2, 0, GRID_H - NA_BAND)
        pattern = jnp.where(i == 0, 0, jnp.where(i == NA_NBLK - 1, 2, 1))
        q0 = pl.multiple_of(i * NA_QBLK, NA_QBLK)
        k0 = pl.multiple_of(CTX_LEN + bs * GRID_W, GRID_W)
        q2 = q_ref[0, pl.ds(CTX_LEN + q0, NA_QBLK), :]
        kw = k_ref[0, pl.ds(k0, NA_BAND_KEYS), :]
        vw = v_ref[0, pl.ds(k0, NA_BAND_KEYS), :]
        kc = k_ref[0, 0:CTX_LEN, :]
        vc = v_ref[0, 0:CTX_LEN, :]
        outs = []
        for h in range(2):
            qb = jnp.where(head0 if h == 0 else ~head0, q2, jnp.zeros_like(q2))
            s_win = lax.dot_general(qb, kw, nt, preferred_element_type=jnp.float32) + bias_ref[h, pattern]
            s_ctx = lax.dot_general(qb, kc, nt, preferred_element_type=jnp.float32)
            m = jnp.maximum(jnp.max(s_win, axis=-1, keepdims=True), jnp.max(s_ctx, axis=-1, keepdims=True))
            e_win = jnp.exp(s_win - m)
            e_ctx = jnp.exp(s_ctx - m)
            den = jnp.sum(e_win, axis=-1, keepdims=True) + jnp.sum(e_ctx, axis=-1, keepdims=True)
            o = (jnp.dot(e_win.astype(jnp.bfloat16), vw, preferred_element_type=jnp.float32)
                 + jnp.dot(e_ctx.astype(jnp.bfloat16), vc, preferred_element_type=jnp.float32))
            outs.append(o / den)
        o_ref[0, pl.ds(q0, NA_QBLK), :] = jnp.where(head0, outs[0], outs[1])
        return carry

    lax.fori_loop(0, NA_NBLK, body, 0)


def _na_call(qkv, bias8):
    n_hp = D_NA // LANES
    return pl.pallas_call(
        _na_kernel,
        out_shape=jax.ShapeDtypeStruct((BATCH, SEQ, D_NA), jnp.float32),
        grid=(n_hp, BATCH),
        in_specs=[pl.BlockSpec((1, T_ALL, LANES), lambda hp, b: (b, 0, hp)),
                  pl.BlockSpec((1, T_ALL, LANES), lambda hp, b: (b, 0, n_hp + hp)),
                  pl.BlockSpec((1, T_ALL, LANES), lambda hp, b: (b, 0, 2 * n_hp + hp)),
                  pl.BlockSpec((2, 2 * NA_KH - 1, GRID_W, GRID_W), lambda hp, b: (hp, 0, 0, 0))],
        out_specs=pl.BlockSpec((1, SEQ, LANES), lambda hp, b: (b, 0, hp)),
        scratch_shapes=[pltpu.VMEM((2, 3, NA_QBLK, NA_BAND_KEYS), jnp.float32)],
        compiler_params=pltpu.CompilerParams(dimension_semantics=("arbitrary", "arbitrary"),
                                             vmem_limit_bytes=VMEM_LIMIT),
        name="na",
    )(qkv, qkv, qkv, bias8)


def _na_bias_table(rpb):
    qc = np.arange(GRID_W)
    cs = np.clip(qc - NA_KW // 2, 0, GRID_W - NA_KW)
    kc = np.arange(GRID_W)
    col_ok = (kc[None, :] >= cs[:, None]) & (kc[None, :] < cs[:, None] + NA_KW)
    dc = np.clip(kc[None, :] - qc[:, None], -(NA_KW - 1), NA_KW - 1) + NA_KW - 1
    return jnp.where(col_ok[None, None], rpb[:, :, dc], NEG).astype(jnp.float32)


def _swap16(x):
    lane = lax.broadcasted_iota(jnp.int32, x.shape, 1)
    return jnp.where((lane & 16) == 0, pltpu.roll(x, LANES - 16, axis=1), pltpu.roll(x, 16, axis=1))


def _rope(x, cos, sin):
    blocks = []
    for i in range(x.shape[1] // LANES):
        xb = x[:, i * LANES:(i + 1) * LANES]
        blocks.append(xb * cos + _swap16(xb) * sin)
    return jnp.concatenate(blocks, axis=1)


def _shift_rows(x, n, fill, down):
    if down:
        return jnp.concatenate([fill, x[:-n]], axis=0)
    return jnp.concatenate([x[n:], fill], axis=0)


def _chunk_cumprod(w, reverse):
    inc = w
    s = 1
    while s < SCAN_CHUNK:
        ones = jnp.ones((s * BATCH, w.shape[1]), jnp.float32)
        inc = inc * _shift_rows(inc, s * BATCH, ones, down=not reverse)
        s *= 2
    exc = _shift_rows(inc, BATCH, jnp.ones((BATCH, w.shape[1]), jnp.float32), down=not reverse)
    return inc, exc


def _scan_tiles(u):
    low = lax.broadcasted_iota(jnp.int32, (BATCH, LANES), 1) < HEAD_DIM
    steps = u.shape[0] // BATCH
    tiles = []
    for m in range(0, steps, 2):
        a = u[m * BATCH:(m + 1) * BATCH]
        b = u[(m + 1) * BATCH:(m + 2) * BATCH] if m + 1 < steps else a
        rows = []
        for hp in range(D_RW // LANES):
            pa = a[:, hp * LANES:(hp + 1) * LANES]
            pb = b[:, hp * LANES:(hp + 1) * LANES]
            rows.append(jnp.where(low, pa, pltpu.roll(pb, HEAD_DIM, axis=1)))
            rows.append(jnp.where(low, pltpu.roll(pa, HEAD_DIM, axis=1), pb))
        wt = jnp.concatenate(rows, axis=0).T
        tiles.append(wt[:HEAD_DIM])
        if m + 1 < steps:
            tiles.append(wt[HEAD_DIM:])
    return tiles


def _prep_kernel(p_ref, pv_ref, nx_ref, cos_ref, sin_ref, mup_ref, mun_ref, w0_ref, wup_ref, a0_ref, aup_ref,
                 gup_ref, kk_ref, ka_ref, rk_ref, bd_ref,
                 v_out, abar_out, rbar_out, ktil_out, btil_out, gend_out, g_out, bonus_out, rows_ref):
    j = pl.program_id(0)
    n_ctx = CTX_LEN // SCAN_CHUNK
    p = p_ref[...]
    seq_start = (j == 0) | (j == n_ctx)
    seq_end = (j == n_ctx - 1) | (j == pl.num_programs(0) - 1)
    prev = _shift_rows(p, BATCH, jnp.where(seq_start, 0.0, pv_ref[...]), down=True)
    nxt = _shift_rows(p, BATCH, jnp.where(seq_end, 0.0, nx_ref[...]), down=False)
    ps = p + mup_ref[...] * (prev - p) + mun_ref[...] * (nxt - p)

    cos = cos_ref[...]
    sin = sin_ref[...]
    r = _rope(ps[:, 0:D_RW], cos, sin)
    k = _rope(ps[:, D_RW:2 * D_RW], cos, sin)
    v = ps[:, 2 * D_RW:3 * D_RW]
    o3 = 3 * D_RW
    wd = ps[:, o3:o3 + 2 * LORA_W]
    ad = ps[:, o3 + 2 * LORA_W:o3 + 2 * LORA_W + 2 * LORA_A]
    gd = ps[:, o3 + 2 * LORA_W + 2 * LORA_A:]

    w_pre = w0_ref[...] + jnp.dot(jnp.tanh(wd).astype(jnp.bfloat16), wup_ref[...], preferred_element_type=jnp.float32)
    decay = jnp.exp(-DECAY_SCALE * _sigmoid(w_pre))
    a = _sigmoid(a0_ref[...] + jnp.dot(ad.astype(jnp.bfloat16), aup_ref[...], preferred_element_type=jnp.float32))
    g = jnp.dot(_sigmoid(gd).astype(jnp.bfloat16), gup_ref[...], preferred_element_type=jnp.float32)

    bd = bd_ref[...]
    kk = k * kk_ref[...]
    kk = kk / jnp.maximum(jnp.sqrt(_group_sum(kk * kk, bd)), 1e-12)
    bonus = _group_sum(r * k * rk_ref[...], bd) * v
    ka = ka_ref[...]

    for out, val in ((g_out, g), (bonus_out, bonus)):
        for cb in range(D_RW // LANES):
            rows_ref[cb] = val[:, cb * LANES:(cb + 1) * LANES]
        for b in range(BATCH):
            for cb in range(D_RW // LANES):
                out[b, :, cb * LANES:(cb + 1) * LANES] = rows_ref[cb, pl.ds(b, SCAN_CHUNK, stride=BATCH), :]

    for t, tile in enumerate(_scan_tiles(v)):
        v_out[t] = tile
    for d in range(2):
        ds_ = slice(d * D_RW, (d + 1) * D_RW)
        a_d = a[:, ds_]
        gam, gam_excl = _chunk_cumprod(decay[:, ds_], reverse=(d == 1))
        inv = 1.0 / gam
        streams = ((abar_out, gam_excl * kk), (rbar_out, gam * r),
                   (ktil_out, k * (1.0 + (a_d - 1.0) * ka) * inv), (btil_out, kk * a_d * inv))
        for out, val in streams:
            for t, tile in enumerate(_scan_tiles(val)):
                out[d, t] = tile
        gam_last = gam[:BATCH] if d == 1 else gam[(SCAN_CHUNK - 1) * BATCH:]
        gend_out[d, 0] = _scan_tiles(gam_last)[0]


def _prep_call(p_rw_t, cos_t, sin_t, mu_prev, mu_next, w0, wup2, a0, aup2, gup, k_k, k_a, r_k, bd):
    rows = SCAN_CHUNK * BATCH
    n_chunks = T_ALL // SCAN_CHUNK
    full = lambda shape: pl.BlockSpec(shape, lambda j: (0,) * len(shape))
    tok = lambda width: pl.BlockSpec((rows, width), lambda j: (j, 0))
    perdir = pl.BlockSpec((2, SCAN_CHUNK, HEAD_DIM, LANES), lambda j: (0, j, 0, 0))
    two = jax.ShapeDtypeStruct((2, T_ALL, HEAD_DIM, LANES), jnp.float32)
    nat = jax.ShapeDtypeStruct((BATCH, SEQ, D_RW), jnp.float32)
    n_ctx = CTX_LEN // SCAN_CHUNK
    lat = pl.BlockSpec((BATCH, SCAN_CHUNK, D_RW), lambda j: (0, jnp.maximum(j - n_ctx, 0), 0))
    return pl.pallas_call(
        _prep_kernel,
        out_shape=(jax.ShapeDtypeStruct((T_ALL, HEAD_DIM, LANES), jnp.float32), two, two, two, two,
                   jax.ShapeDtypeStruct((2, n_chunks, HEAD_DIM, LANES), jnp.float32), nat, nat),
        grid=(n_chunks,),
        in_specs=[tok(D_RW_IN),
                  pl.BlockSpec((BATCH, D_RW_IN), lambda j: (jnp.maximum(j * SCAN_CHUNK - 1, 0), 0)),
                  pl.BlockSpec((BATCH, D_RW_IN), lambda j: (jnp.minimum((j + 1) * SCAN_CHUNK, T_ALL - 1), 0)),
                  tok(LANES), tok(LANES),
                  full((1, D_RW_IN)), full((1, D_RW_IN)),
                  full((1, 2 * D_RW)), full((2 * LORA_W, 2 * D_RW)),
                  full((1, 2 * D_RW)), full((2 * LORA_A, 2 * D_RW)),
                  full((LORA_G, D_RW)),
                  full((1, D_RW)), full((1, D_RW)), full((1, D_RW)),
                  full((D_RW, D_RW))],
        out_specs=(pl.BlockSpec((SCAN_CHUNK, HEAD_DIM, LANES), lambda j: (j, 0, 0)), perdir, perdir, perdir, perdir,
                   pl.BlockSpec((2, 1, HEAD_DIM, LANES), lambda j: (0, j, 0, 0)), lat, lat),
        scratch_shapes=[pltpu.VMEM((D_RW // LANES, SCAN_CHUNK * BATCH, LANES), jnp.float32)],
        compiler_params=pltpu.CompilerParams(dimension_semantics=("arbitrary",), vmem_limit_bytes=VMEM_LIMIT),
        name="prep",
    )(p_rw_t, p_rw_t, p_rw_t, cos_t, sin_t, mu_prev, mu_next, w0, wup2, a0, aup2, gup, k_k, k_a, r_k, bd)


def _rope_tables():
    nf = HEAD_DIM // 4
    pos = np.arange(SEQ)
    inv = ROPE_THETA ** (-np.arange(nf, dtype=np.float32) / nf)
    lane = np.arange(LANES) % HEAD_DIM
    half, pair, f = lane // 32, (lane % 32) // 16, lane % 16
    coord = np.where(half[None, :] == 0, (pos // GRID_W)[:, None], (pos % GRID_W)[:, None]).astype(np.float32)
    ang = coord * inv[f][None, :].astype(np.float32)
    cos = np.cos(ang).astype(np.float32)
    sin = np.sin(ang).astype(np.float32) * np.where(pair == 0, -1.0, 1.0)[None, :].astype(np.float32)
    cos = np.concatenate([np.ones((CTX_LEN, LANES), np.float32), cos], axis=0)
    sin = np.concatenate([np.zeros((CTX_LEN, LANES), np.float32), sin], axis=0)
    return jnp.asarray(np.repeat(cos, BATCH, axis=0)), jnp.asarray(np.repeat(sin, BATCH, axis=0))


def _scan_kernel(v_ref, abar_ref, rbar_ref, ktil_ref, btil_ref, gend_ref, o_ref, s_ref):
    d = pl.program_id(0)
    i = pl.program_id(1)

    @pl.when(i == 0)
    def _():
        s_ref[...] = jnp.zeros_like(s_ref)

    k_unroll = 32
    n_kb = HEAD_DIM // k_unroll
    zero = jnp.zeros((HEAD_DIM, LANES), jnp.float32)

    def make_step(with_out):
        def step(s, carry):
            t = jnp.where(d == 0, s, SCAN_CHUNK - 1 - s)
            vv = v_ref[t]

            def sk_body(kb, sk):
                for u in range(k_unroll):
                    k = kb * k_unroll + u
                    sk = sk + s_ref[k] * abar_ref[0, t, pl.ds(k, 1), :]
                return sk

            sk = lax.fori_loop(0, n_kb, sk_body, zero)

            def upd_body(kb, o):
                for u in range(k_unroll):
                    k = kb * k_unroll + u
                    s_new = s_ref[k] + (vv * ktil_ref[0, t, pl.ds(k, 1), :] - sk * btil_ref[0, t, pl.ds(k, 1), :])
                    s_ref[k] = s_new
                    if with_out:
                        o = o + s_new * rbar_ref[0, t, pl.ds(k, 1), :]
                return o

            o = lax.fori_loop(0, n_kb, upd_body, zero)
            if with_out:
                o_ref[0, t] = o
            return carry
        return step

    @pl.when(i < CTX_LEN // SCAN_CHUNK)
    def _():
        lax.fori_loop(0, SCAN_CHUNK, make_step(False), 0)

    @pl.when(i >= CTX_LEN // SCAN_CHUNK)
    def _():
        lax.fori_loop(0, SCAN_CHUNK, make_step(True), 0)

    def renorm(kb, carry):
        for u in range(k_unroll):
            k = kb * k_unroll + u
            s_ref[k] = s_ref[k] * gend_ref[0, 0, pl.ds(k, 1), :]
        return carry

    lax.fori_loop(0, n_kb, renorm, 0)


def _scan_call(v_t, abar_t, rbar_t, ktil_t, btil_t, gend_t):
    nc = CTX_LEN // SCAN_CHUNK
    nl = SEQ // SCAN_CHUNK

    def blk(d, i):
        rev = jnp.where(i < nc, nc - 1 - i, 2 * nc + nl - 1 - i)
        return jnp.where(d == 0, i, rev)

    def oblk(d, i):
        fwd = jnp.maximum(i - nc, 0)
        rev = jnp.where(i < nc, nl - 1, nc + nl - 1 - i)
        return jnp.where(d == 0, fwd, rev)

    shared = pl.BlockSpec((SCAN_CHUNK, HEAD_DIM, LANES), lambda d, i: (blk(d, i), 0, 0))
    perdir = pl.BlockSpec((1, SCAN_CHUNK, HEAD_DIM, LANES), lambda d, i: (d, blk(d, i), 0, 0))
    perchunk = pl.BlockSpec((1, 1, HEAD_DIM, LANES), lambda d, i: (d, blk(d, i), 0, 0))
    return pl.pallas_call(
        _scan_kernel,
        out_shape=jax.ShapeDtypeStruct((2, SEQ, HEAD_DIM, LANES), jnp.float32),
        grid=(2, nc + nl),
        in_specs=[shared, perdir, perdir, perdir, perdir, perchunk],
        out_specs=pl.BlockSpec((1, SCAN_CHUNK, HEAD_DIM, LANES), lambda d, i: (d, oblk(d, i), 0, 0)),
        scratch_shapes=[pltpu.VMEM((HEAD_DIM, HEAD_DIM, LANES), jnp.float32)],
        compiler_params=pltpu.CompilerParams(dimension_semantics=("arbitrary", "arbitrary"),
                                             vmem_limit_bytes=VMEM_LIMIT),
        name="scan",
    )(v_t, abar_t, rbar_t, ktil_t, btil_t, gend_t)


def _from_scan_layout(o):
    return o.reshape(SEQ, HEAD_DIM, H_RW, BATCH).transpose(3, 0, 2, 1).reshape(BATCH, SEQ, D_RW)


def _finish_kernel(na_ref, o_ref, bonus_ref, g_ref, x_ref, mod_ref, lng_ref, lnb_ref, wout_ref, n2g_ref, bd_ref,
                   wr_ref, br_ref, uj_ref, x1_out, h2_out, eid_out, wt_out, rank_out, cnt_out, run_ref):
    @pl.when((pl.program_id(0) == 0) & (pl.program_id(1) == 0))
    def _():
        run_ref[...] = jnp.zeros_like(run_ref)

    bd = bd_ref[...]
    o = o_ref[0]
    mu = _group_sum(o, bd) * (1.0 / HEAD_DIM)
    oc = o - mu
    var = _group_sum(oc * oc, bd) * (1.0 / HEAD_DIM)
    y = oc * lax.rsqrt(var + RW_LN_EPS) * lng_ref[...] + lnb_ref[...]
    rw = (y + bonus_ref[0]) * g_ref[0]
    mix = jnp.concatenate([na_ref[0], rw], axis=-1).astype(jnp.bfloat16)
    yx = jnp.dot(mix, wout_ref[...], preferred_element_type=jnp.float32)
    mod = mod_ref[0]
    x1 = x_ref[0] + mod[2:3, :] * yx
    x1_out[0] = x1
    h2 = _rmsnorm_rows(x1, n2g_ref[...]) * (1.0 + mod[4:5, :]) + mod[3:4, :]
    h2_out[0] = h2.astype(jnp.bfloat16)

    lg = lax.dot_general(wr_ref[...], h2, (((1,), (1,)), ((), ())), precision=_HI,
                         preferred_element_type=jnp.float32) + br_ref[...]
    gl = lg[0:MOE_GROUPS, :]
    grow = lax.broadcasted_iota(jnp.int32, gl.shape, 0)
    gmax = jnp.max(gl, axis=0, keepdims=True)
    g_sel = jnp.min(jnp.where(gl == gmax, grow, MOE_GROUPS), axis=0, keepdims=True)
    p_group = 1.0 / jnp.sum(jnp.exp(gl - gmax), axis=0, keepdims=True)
    el = jnp.zeros((MOE_PER_GROUP, FIN_TILE), jnp.float32)
    for gi in range(MOE_GROUPS):
        el = jnp.where(g_sel == gi, lg[8 + gi * MOE_PER_GROUP:8 + (gi + 1) * MOE_PER_GROUP, :], el)
    erow = lax.broadcasted_iota(jnp.int32, el.shape, 0)
    m1 = jnp.max(el, axis=0, keepdims=True)
    i1 = jnp.min(jnp.where(el == m1, erow, MOE_PER_GROUP), axis=0, keepdims=True)
    el2 = jnp.where(erow == i1, -jnp.inf, el)
    m2 = jnp.max(el2, axis=0, keepdims=True)
    i2 = jnp.min(jnp.where(el2 == m2, erow, MOE_PER_GROUP), axis=0, keepdims=True)
    e21 = jnp.exp(m2 - m1)
    e0 = g_sel * MOE_PER_GROUP + i1
    e1 = g_sel * MOE_PER_GROUP + i2
    eid_out[0:1, :] = e0
    eid_out[1:2, :] = e1
    wt_out[0:1, :] = p_group / (1.0 + e21)
    wt_out[1:2, :] = p_group * e21 / (1.0 + e21)

    xrow = lax.broadcasted_iota(jnp.int32, (N_EXPERTS, FIN_TILE), 0)
    m0 = xrow == e0
    m1b = xrow == e1
    c0 = jnp.dot(jnp.where(m0, 1.0, 0.0).astype(jnp.bfloat16), uj_ref[...], preferred_element_type=jnp.float32)
    c1 = jnp.dot(jnp.where(m1b, 1.0, 0.0).astype(jnp.bfloat16), uj_ref[...], preferred_element_type=jnp.float32)
    run = run_ref[...]
    tot0 = c0[:, FIN_TILE:]
    rank_out[0:1, :] = jnp.sum(jnp.where(m0, run + c0[:, :FIN_TILE], 0.0), axis=0, keepdims=True).astype(jnp.int32)
    rank_out[1:2, :] = jnp.sum(jnp.where(m1b, run + tot0 + c1[:, :FIN_TILE], 0.0), axis=0,
                               keepdims=True).astype(jnp.int32)
    run = run + tot0 + c1[:, FIN_TILE:]
    run_ref[...] = run
    cnt_out[...] = run

def _finish_call(part, na_x, o_rw, bonus, g, x, mod3, ln_g, ln_b, w_out_bf16, norm2_g, bd, wr_t, br, uj):
    n_lat = SEQ // FIN_TILE
    b0 = part * PART_BATCH
    full = lambda shape: pl.BlockSpec(shape, lambda b, j: (0,) * len(shape))
    src = lambda width: pl.BlockSpec((1, FIN_TILE, width), lambda b, j: (b0 + b, j, 0))
    lat = lambda width: pl.BlockSpec((1, FIN_TILE, width), lambda b, j: (b, j, 0))
    tokl = pl.BlockSpec((2, FIN_TILE), lambda b, j: (0, b * n_lat + j))
    return pl.pallas_call(
        _finish_kernel,
        out_shape=(jax.ShapeDtypeStruct((PART_BATCH, SEQ, D_MODEL), jnp.float32),
                   jax.ShapeDtypeStruct((PART_BATCH, SEQ, D_MODEL), jnp.bfloat16),
                   jax.ShapeDtypeStruct((2, N_TOK), jnp.int32),
                   jax.ShapeDtypeStruct((2, N_TOK), jnp.float32),
                   jax.ShapeDtypeStruct((2, N_TOK), jnp.int32),
                   jax.ShapeDtypeStruct((N_EXPERTS, FIN_TILE), jnp.float32)),
        grid=(PART_BATCH, n_lat),
        in_specs=[src(D_NA), src(D_RW), src(D_RW), src(D_RW), src(D_MODEL),
                  pl.BlockSpec((1, 6, D_MODEL), lambda b, j: (b0 + b, 0, 0)),
                  full((1, D_RW)), full((1, D_RW)), full((D_MODEL, D_MODEL)), full((1, D_MODEL)),
                  full((D_RW, D_RW)), full((LOGIT_ROWS, D_MODEL)), full((LOGIT_ROWS, 1)),
                  full((FIN_TILE, 2 * FIN_TILE))],
        out_specs=(lat(D_MODEL), lat(D_MODEL), tokl, tokl, tokl, full((N_EXPERTS, FIN_TILE))),
        scratch_shapes=[pltpu.VMEM((N_EXPERTS, FIN_TILE), jnp.float32)],
        compiler_params=pltpu.CompilerParams(dimension_semantics=("arbitrary", "arbitrary"),
                                             vmem_limit_bytes=VMEM_LIMIT),
        name="finish",
    )(na_x, o_rw, bonus, g, x, mod3, ln_g, ln_b, w_out_bf16, norm2_g, bd, wr_t, br, uj)


def _moe_kernel(te_ref, tv_ref, tx_ref, xs_ref, w1_ref, w3_ref, w2_ref, y_ref, w1s, w3s, w2s):
    t = pl.program_id(0)

    @pl.when((t == 0) | (te_ref[t] != te_ref[jnp.maximum(t - 1, 0)]))
    def _():
        w1s[...] = w1_ref[0].astype(jnp.bfloat16)
        w3s[...] = w3_ref[0].astype(jnp.bfloat16)
        w2s[...] = w2_ref[0].astype(jnp.bfloat16)

    @pl.when(tv_ref[t] == 1)
    def _():
        xs = xs_ref[...]
        h1 = jnp.dot(xs, w1s[...], preferred_element_type=jnp.float32)
        h3 = jnp.dot(xs, w3s[...], preferred_element_type=jnp.float32)
        he = (_silu(h1) * h3).astype(jnp.bfloat16)
        y_ref[...] = jnp.dot(he, w2s[...], preferred_element_type=jnp.float32).astype(jnp.bfloat16)

    @pl.when(tv_ref[t] == 0)
    def _():
        y_ref[...] = jnp.zeros_like(y_ref)


def _moe_call(tile_e, tile_valid, tile_src, xs, w1, w3, w2):
    return pl.pallas_call(
        _moe_kernel,
        out_shape=jax.ShapeDtypeStruct((N_SLOTS, D_MODEL), jnp.bfloat16),
        grid_spec=pltpu.PrefetchScalarGridSpec(
            num_scalar_prefetch=3,
            grid=(N_MOE_TILES,),
            in_specs=[pl.BlockSpec((MOE_TILE, D_MODEL), lambda t, te, tv, tx: (tx[t], 0)),
                      pl.BlockSpec((1, D_MODEL, D_EXPERT), lambda t, te, tv, tx: (te[t], 0, 0)),
                      pl.BlockSpec((1, D_MODEL, D_EXPERT), lambda t, te, tv, tx: (te[t], 0, 0)),
                      pl.BlockSpec((1, D_EXPERT, D_MODEL), lambda t, te, tv, tx: (te[t], 0, 0))],
            out_specs=pl.BlockSpec((MOE_TILE, D_MODEL), lambda t, te, tv, tx: (t, 0)),
            scratch_shapes=[pltpu.VMEM((D_MODEL, D_EXPERT), jnp.bfloat16),
                            pltpu.VMEM((D_MODEL, D_EXPERT), jnp.bfloat16),
                            pltpu.VMEM((D_EXPERT, D_MODEL), jnp.bfloat16)]),
        compiler_params=pltpu.CompilerParams(dimension_semantics=("arbitrary",), vmem_limit_bytes=VMEM_LIMIT),
        name="moe",
    )(tile_e, tile_valid, tile_src, xs, w1, w3, w2)


def _moe_plan(eid, rank, cnt):
    counts = cnt[:, 0].astype(jnp.int32)
    gsz = ((counts + MOE_TILE - 1) // MOE_TILE) * MOE_TILE
    gend = jnp.cumsum(gsz)
    goff = gend - gsz
    pos = rank
    for e in range(N_EXPERTS):
        pos = pos + jnp.where(eid == e, goff[e], 0)
    starts = jnp.arange(N_MOE_TILES, dtype=jnp.int32) * MOE_TILE
    te = jnp.sum((gend[None, :] <= starts[:, None]).astype(jnp.int32), axis=1)
    valid = (te < N_EXPERTS).astype(jnp.int32)
    last = jnp.max(jnp.where(counts > 0, jnp.arange(N_EXPERTS), 0)).astype(jnp.int32)
    te = jnp.where(valid == 1, te, last)
    n_valid = gend[-1] // MOE_TILE
    tx = jnp.clip(jnp.arange(N_MOE_TILES, dtype=jnp.int32), 0, jnp.maximum(n_valid - 1, 0))
    tok = jnp.broadcast_to(jnp.arange(N_TOK, dtype=jnp.int32), (2, N_TOK))
    src_tok = jnp.zeros((N_SLOTS,), jnp.int32).at[pos.reshape(-1)].set(tok.reshape(-1))
    return pos, src_tok, te, valid, tx


def _final_kernel(*refs):
    mod_ref, o_ref = refs[-2], refs[-1]
    part = pl.program_id(0) // PART_BATCH
    for p in range(MOE_PARTS):
        x1_ref, ya_ref, yb_ref, wt_ref = refs[4 * p:4 * p + 4]

        @pl.when(part == p)
        def _():
            w = wt_ref[...]
            ya = ya_ref[0].astype(jnp.float32)
            yb = yb_ref[0].astype(jnp.float32)
            o_ref[0] = x1_ref[0] + mod_ref[0][5:6, :] * (w[:, 0:1] * ya + w[:, 1:2] * yb)


def _final_call(parts, mod3):
    n_lat = SEQ // TOK_TILE
    in_specs, args = [], []
    for p, arrs in enumerate(parts):
        def own(b, j, p=p):
            active = b // PART_BATCH == p
            return jnp.where(active, b - p * PART_BATCH, 0), jnp.where(active, j, 0)
        lat = pl.BlockSpec((1, TOK_TILE, D_MODEL), lambda b, j, own=own: (*own(b, j), 0))
        wts = pl.BlockSpec((TOK_TILE, 2), lambda b, j, own=own: (own(b, j)[0] * n_lat + own(b, j)[1], 0))
        in_specs += [lat, lat, lat, wts]
        args += list(arrs)
    in_specs.append(pl.BlockSpec((1, 6, D_MODEL), lambda b, j: (b, 0, 0)))
    return pl.pallas_call(
        _final_kernel,
        out_shape=jax.ShapeDtypeStruct((BATCH, SEQ, D_MODEL), jnp.float32),
        grid=(BATCH, n_lat),
        in_specs=in_specs,
        out_specs=pl.BlockSpec((1, TOK_TILE, D_MODEL), lambda b, j: (b, j, 0)),
        name="final",
    )(*args, mod3)


def _block_diag2(w):
    z = jnp.zeros_like(w[0])
    return jnp.concatenate([jnp.concatenate([w[0], z], axis=1), jnp.concatenate([z, w[1]], axis=1)], axis=0)


def kernel(x, c, ctx, c_ctx, w_mod, b_mod, norm1_g, norm2_g, w_in, na_q_g, na_k_g, na_rpb, rw_mu_prev, rw_mu_next,
           rw_w0, rw_w_up, rw_a0, rw_a_up, rw_g_up, rw_k_k, rw_k_a, rw_r_k, rw_ln_g, rw_ln_b, w_out, moe_wg, moe_bg,
           moe_we, moe_be, moe_w1, moe_w3, moe_w2):
    bf = jnp.bfloat16
    mod_rows = BATCH + 8
    cs = jnp.concatenate([c, c_ctx[None, :], jnp.zeros((mod_rows - BATCH - 1, D_MODEL), jnp.float32)], axis=0)
    mod = _mod_call(cs, w_mod[0], b_mod[0][None, :])
    mod3 = mod.reshape(mod_rows, 6, D_MODEL)

    head = jnp.arange(D_RW) // HEAD_DIM
    bd = (head[:, None] == head[None, :]).astype(bf)
    qkv, p_rw = _inproj_call(x, ctx, mod3, norm1_g[0][None, :], w_in[0].astype(bf),
                             jnp.tile(na_q_g[0], H_NA)[None, :], jnp.tile(na_k_g[0], H_NA)[None, :], bd)

    na_x = _na_call(qkv, _na_bias_table(na_rpb[0]))

    cos_t, sin_t = _rope_tables()
    p_rw_t = p_rw.transpose(1, 0, 2).reshape(T_ALL * BATCH, D_RW_IN)
    v, abar, rbar, ktil, btil, gend, g, bonus = _prep_call(
        p_rw_t, cos_t, sin_t, rw_mu_prev[0][None, :], rw_mu_next[0][None, :],
        rw_w0[0].reshape(1, 2 * D_RW), _block_diag2(rw_w_up[0]).astype(bf),
        rw_a0[0].reshape(1, 2 * D_RW), _block_diag2(rw_a_up[0]).astype(bf),
        rw_g_up[0].astype(bf), rw_k_k[0][None, :], rw_k_a[0][None, :], rw_r_k[0].reshape(1, D_RW), bd)

    o2 = _scan_call(v, abar, rbar, ktil, btil, gend)
    o_rw = _from_scan_layout(o2[0] + o2[1])

    wr_t = jnp.zeros((LOGIT_ROWS, D_MODEL), jnp.float32)
    wr_t = wr_t.at[0:MOE_GROUPS].set(moe_wg[0].T)
    wr_t = wr_t.at[8:].set(moe_we[0].transpose(0, 2, 1).reshape(N_EXPERTS, D_MODEL))
    br = jnp.zeros((LOGIT_ROWS,), jnp.float32).at[0:MOE_GROUPS].set(moe_bg[0]).at[8:].set(moe_be[0].reshape(-1))
    tri = jnp.arange(FIN_TILE)
    uj = jnp.concatenate([(tri[:, None] < tri[None, :]).astype(bf), jnp.ones((FIN_TILE, FIN_TILE), bf)], axis=1)
    w_out_bf = w_out[0].astype(bf)
    staged = []
    for part in range(MOE_PARTS):
        x1, h2p, eid, wts, rank, cnt = _finish_call(part, na_x, o_rw, bonus, g, x, mod3, rw_ln_g[0][None, :],
                                                    rw_ln_b[0][None, :], w_out_bf, norm2_g[0][None, :], bd,
                                                    wr_t, br[:, None], uj)
        pos, src_tok, tile_e, tile_valid, tile_src = _moe_plan(eid, rank, cnt)
        xs = jnp.take(h2p.reshape(N_TOK, D_MODEL), src_tok, axis=0, mode="clip")
        staged.append((x1, wts, pos, xs, tile_e, tile_valid, tile_src))
    parts = []
    for x1, wts, pos, xs, tile_e, tile_valid, tile_src in staged:
        ys = _moe_call(tile_e, tile_valid, tile_src, xs, moe_w1[0], moe_w3[0], moe_w2[0])
        ya = jnp.take(ys, pos[0], axis=0, mode="clip").reshape(PART_BATCH, SEQ, D_MODEL)
        yb = jnp.take(ys, pos[1], axis=0, mode="clip").reshape(PART_BATCH, SEQ, D_MODEL)
        parts.append((x1, ya, yb, wts.T))
    return _final_call(parts, mod3)
```

```python
import functools
import math

import jax
import jax.numpy as jnp
import numpy as np
from jax import lax
from jax.experimental import pallas as pl
from jax.experimental.pallas import tpu as pltpu

D_MODEL = 1024
BATCH = 16
SEQ = 2048
GRID_W = 64
GRID_H = SEQ // GRID_W
CTX_LEN = 256
T_ALL = CTX_LEN + SEQ
HEAD_DIM = 64
D_NA = 512
D_RW = 512
H_NA = D_NA // HEAD_DIM
H_RW = D_RW // HEAD_DIM
NA_KH = 8
NA_KW = 16
LORA_W = 64
LORA_A = 64
LORA_G = 128
D_RW_IN = 3 * D_RW + 2 * (LORA_W + LORA_A) + LORA_G
D_IN = 3 * D_NA + D_RW_IN
MOE_GROUPS = 4
MOE_PER_GROUP = 8
N_EXPERTS = MOE_GROUPS * MOE_PER_GROUP
D_EXPERT = 512
ROPE_THETA = 10000.0
NORM_EPS = 1e-6
RW_LN_EPS = 64e-5
NEG = -1e30
DECAY_SCALE = math.exp(-0.5)

LANES = 128
TOK_TILE = 256
FIN_TILE = 512
N_TILES = T_ALL // TOK_TILE
SCAN_CHUNK = 16
MOE_TILE = 256
N_TOK = BATCH * SEQ
N_SLOTS = 2 * N_TOK + N_EXPERTS * MOE_TILE
N_MOE_TILES = N_SLOTS // MOE_TILE
LOGIT_ROWS = 8 + N_EXPERTS
VMEM_LIMIT = 48 * 1024 * 1024

_HI = lax.Precision.HIGHEST


def _sigmoid(x):
    return 1.0 / (1.0 + jnp.exp(-x))


def _silu(x):
    return x * _sigmoid(x)


def _rmsnorm_rows(xf, g):
    return xf * lax.rsqrt(jnp.mean(xf * xf, axis=-1, keepdims=True) + NORM_EPS) * g


def _group_sum(x, bd):
    hi = x.astype(jnp.bfloat16)
    lo = (x - hi.astype(jnp.float32)).astype(jnp.bfloat16)
    return (jnp.dot(hi, bd, preferred_element_type=jnp.float32)
            + jnp.dot(lo, bd, preferred_element_type=jnp.float32))


def _mod_kernel(c_ref, w_ref, b_ref, o_ref):
    o_ref[...] = jnp.dot(_silu(c_ref[...]), w_ref[...], precision=_HI,
                         preferred_element_type=jnp.float32) + b_ref[...]


def _mod_call(cs, w_mod, b_mod):
    rows = cs.shape[0]
    n = w_mod.shape[1]
    blk = 1024
    return pl.pallas_call(
        _mod_kernel,
        out_shape=jax.ShapeDtypeStruct((rows, n), jnp.float32),
        grid=(n // blk,),
        in_specs=[pl.BlockSpec((rows, D_MODEL), lambda j: (0, 0)),
                  pl.BlockSpec((D_MODEL, blk), lambda j: (0, j)),
                  pl.BlockSpec((1, blk), lambda j: (0, j))],
        out_specs=pl.BlockSpec((rows, blk), lambda j: (0, j)),
        name="mod",
    )(cs, w_mod, b_mod)


def _inproj_kernel(x_ref, ctx_ref, modx_ref, modc_ref, g_ref, w_ref, qg_ref, kg_ref, bd_ref, na_ref, rw_ref):
    is_ctx = pl.program_id(1) == 0
    xin = jnp.where(is_ctx, ctx_ref[0], x_ref[0])
    mod = jnp.where(is_ctx, modc_ref[0], modx_ref[0])
    h = _rmsnorm_rows(xin, g_ref[...]) * (1.0 + mod[1:2, :]) + mod[0:1, :]
    p = jnp.dot(h.astype(jnp.bfloat16), w_ref[...], preferred_element_type=jnp.float32)
    rw_ref[0] = p[:, 3 * D_NA:].astype(jnp.bfloat16)
    bd = bd_ref[...]
    q = p[:, 0:D_NA]
    k = p[:, D_NA:2 * D_NA]
    qn = q * lax.rsqrt(_group_sum(q * q, bd) * (1.0 / HEAD_DIM) + NORM_EPS) * (qg_ref[...] * HEAD_DIM ** -0.5)
    kn = k * lax.rsqrt(_group_sum(k * k, bd) * (1.0 / HEAD_DIM) + NORM_EPS) * kg_ref[...]
    na_ref[0, :, 0:D_NA] = qn.astype(jnp.bfloat16)
    na_ref[0, :, D_NA:2 * D_NA] = kn.astype(jnp.bfloat16)
    na_ref[0, :, 2 * D_NA:] = p[:, 2 * D_NA:3 * D_NA].astype(jnp.bfloat16)


def _inproj_call(x, ctx, mod3, norm_g, w_in_bf16, q_g, k_g, bd):
    return pl.pallas_call(
        _inproj_kernel,
        out_shape=(jax.ShapeDtypeStruct((BATCH, T_ALL, 3 * D_NA), jnp.bfloat16),
                   jax.ShapeDtypeStruct((BATCH, T_ALL, D_RW_IN), jnp.bfloat16)),
        grid=(BATCH, N_TILES),
        in_specs=[pl.BlockSpec((1, TOK_TILE, D_MODEL), lambda b, j: (b, jnp.maximum(j - 1, 0), 0)),
                  pl.BlockSpec((1, TOK_TILE, D_MODEL), lambda b, j: (b, 0, 0)),
                  pl.BlockSpec((1, 6, D_MODEL), lambda b, j: (b, 0, 0)),
                  pl.BlockSpec((1, 6, D_MODEL), lambda b, j: (BATCH, 0, 0)),
                  pl.BlockSpec((1, D_MODEL), lambda b, j: (0, 0)),
                  pl.BlockSpec((D_MODEL, D_IN), lambda b, j: (0, 0)),
                  pl.BlockSpec((1, D_NA), lambda b, j: (0, 0)),
                  pl.BlockSpec((1, D_NA), lambda b, j: (0, 0)),
                  pl.BlockSpec((D_NA, D_NA), lambda b, j: (0, 0))],
        out_specs=(pl.BlockSpec((1, TOK_TILE, 3 * D_NA), lambda b, j: (b, j, 0)),
                   pl.BlockSpec((1, TOK_TILE, D_RW_IN), lambda b, j: (b, j, 0))),
        compiler_params=pltpu.CompilerParams(vmem_limit_bytes=VMEM_LIMIT),
        name="inproj",
    )(x, ctx, mod3, mod3, norm_g, w_in_bf16, q_g, k_g, bd)


NA_QROWS = 4
NA_QBLK = NA_QROWS * GRID_W
NA_BAND = NA_KH + NA_QROWS - 1
NA_BAND_KEYS = NA_BAND * GRID_W
NA_NBLK = GRID_H // NA_QROWS


def _na_band_start(i):
    return np.clip(i * NA_QROWS - NA_KH // 2, 0, GRID_H - NA_BAND)


def _na_kernel(q_ref, k_ref, v_ref, tz_ref, o_ref, bias_ref):
    @pl.when(pl.program_id(1) == 0)
    def _():
        masked = jnp.full((GRID_W, GRID_W), NEG, jnp.float32)
        for h in range(2):
            for p, i in enumerate((0, 1, NA_NBLK - 1)):
                bs = int(_na_band_start(i))
                for g in range(NA_QROWS):
                    qr = i * NA_QROWS + g
                    rs = int(np.clip(qr - NA_KH // 2, 0, GRID_H - NA_KH))
                    for j in range(NA_BAND):
                        kr = bs + j
                        blk = tz_ref[h, kr - qr + NA_KH - 1] if rs <= kr < rs + NA_KH else masked
                        bias_ref[h, p, g * GRID_W:(g + 1) * GRID_W, j * GRID_W:(j + 1) * GRID_W] = blk

    nt = (((1,), (1,)), ((), ()))
    head0 = lax.broadcasted_iota(jnp.int32, (NA_QBLK, LANES), 1) < HEAD_DIM

    def body(i, carry):
        bs = jnp.clip(i * NA_QROWS - NA_KH // 2, 0, GRID_H - NA_BAND)
        pattern = jnp.where(i == 0, 0, jnp.where(i == NA_NBLK - 1, 2, 1))
        q0 = pl.multiple_of(i * NA_QBLK, NA_QBLK)
        k0 = pl.multiple_of(CTX_LEN + bs * GRID_W, GRID_W)
        q2 = q_ref[0, pl.ds(CTX_LEN + q0, NA_QBLK), :]
        kw = k_ref[0, pl.ds(k0, NA_BAND_KEYS), :]
        vw = v_ref[0, pl.ds(k0, NA_BAND_KEYS), :]
        kc = k_ref[0, 0:CTX_LEN, :]
        vc = v_ref[0, 0:CTX_LEN, :]
        outs = []
        for h in range(2):
            qb = jnp.where(head0 if h == 0 else ~head0, q2, jnp.zeros_like(q2))
            s_win = lax.dot_general(qb, kw, nt, preferred_element_type=jnp.float32) + bias_ref[h, pattern]
            s_ctx = lax.dot_general(qb, kc, nt, preferred_element_type=jnp.float32)
            m = jnp.maximum(jnp.max(s_win, axis=-1, keepdims=True), jnp.max(s_ctx, axis=-1, keepdims=True))
            e_win = jnp.exp(s_win - m)
            e_ctx = jnp.exp(s_ctx - m)
            den = jnp.sum(e_win, axis=-1, keepdims=True) + jnp.sum(e_ctx, axis=-1, keepdims=True)
            o = (jnp.dot(e_win.astype(jnp.bfloat16), vw, preferred_element_type=jnp.float32)
                 + jnp.dot(e_ctx.astype(jnp.bfloat16), vc, preferred_element_type=jnp.float32))
            outs.append(o / den)
        o_ref[0, pl.ds(q0, NA_QBLK), :] = jnp.where(head0, outs[0], outs[1])
        return carry

    lax.fori_loop(0, NA_NBLK, body, 0)


def _na_call(qkv, bias8):
    n_hp = D_NA // LANES
    return pl.pallas_call(
        _na_kernel,
        out_shape=jax.ShapeDtypeStruct((BATCH, SEQ, D_NA), jnp.float32),
        grid=(n_hp, BATCH),
        in_specs=[pl.BlockSpec((1, T_ALL, LANES), lambda hp, b: (b, 0, hp)),
                  pl.BlockSpec((1, T_ALL, LANES), lambda hp, b: (b, 0, n_hp + hp)),
                  pl.BlockSpec((1, T_ALL, LANES), lambda hp, b: (b, 0, 2 * n_hp + hp)),
                  pl.BlockSpec((2, 2 * NA_KH - 1, GRID_W, GRID_W), lambda hp, b: (hp, 0, 0, 0))],
        out_specs=pl.BlockSpec((1, SEQ, LANES), lambda hp, b: (b, 0, hp)),
        scratch_shapes=[pltpu.VMEM((2, 3, NA_QBLK, NA_BAND_KEYS), jnp.float32)],
        compiler_params=pltpu.CompilerParams(dimension_semantics=("arbitrary", "arbitrary"),
                                             vmem_limit_bytes=VMEM_LIMIT),
        name="na",
    )(qkv, qkv, qkv, bias8)


def _na_bias_table(rpb):
    qc = np.arange(GRID_W)
    cs = np.clip(qc - NA_KW // 2, 0, GRID_W - NA_KW)
    kc = np.arange(GRID_W)
    col_ok = (kc[None, :] >= cs[:, None]) & (kc[None, :] < cs[:, None] + NA_KW)
    dc = np.clip(kc[None, :] - qc[:, None], -(NA_KW - 1), NA_KW - 1) + NA_KW - 1
    return jnp.where(col_ok[None, None], rpb[:, :, dc], NEG).astype(jnp.float32)


def _swap16(x):
    lane = lax.broadcasted_iota(jnp.int32, x.shape, 1)
    return jnp.where((lane & 16) == 0, pltpu.roll(x, LANES - 16, axis=1), pltpu.roll(x, 16, axis=1))


def _rope(x, cos, sin):
    blocks = []
    for i in range(x.shape[1] // LANES):
        xb = x[:, i * LANES:(i + 1) * LANES]
        blocks.append(xb * cos + _swap16(xb) * sin)
    return jnp.concatenate(blocks, axis=1)


def _shift_rows(x, n, fill, down):
    if down:
        return jnp.concatenate([fill, x[:-n]], axis=0)
    return jnp.concatenate([x[n:], fill], axis=0)


def _chunk_cumprod(w, reverse):
    inc = w
    s = 1
    while s < SCAN_CHUNK:
        ones = jnp.ones((s * BATCH, w.shape[1]), jnp.float32)
        inc = inc * _shift_rows(inc, s * BATCH, ones, down=not reverse)
        s *= 2
    exc = _shift_rows(inc, BATCH, jnp.ones((BATCH, w.shape[1]), jnp.float32), down=not reverse)
    return inc, exc


def _scan_tiles(u):
    low = lax.broadcasted_iota(jnp.int32, (BATCH, LANES), 1) < HEAD_DIM
    steps = u.shape[0] // BATCH
    tiles = []
    for m in range(0, steps, 2):
        a = u[m * BATCH:(m + 1) * BATCH]
        b = u[(m + 1) * BATCH:(m + 2) * BATCH] if m + 1 < steps else a
        rows = []
        for hp in range(D_RW // LANES):
            pa = a[:, hp * LANES:(hp + 1) * LANES]
            pb = b[:, hp * LANES:(hp + 1) * LANES]
            rows.append(jnp.where(low, pa, pltpu.roll(pb, HEAD_DIM, axis=1)))
            rows.append(jnp.where(low, pltpu.roll(pa, HEAD_DIM, axis=1), pb))
        wt = jnp.concatenate(rows, axis=0).T
        tiles.append(wt[:HEAD_DIM])
        if m + 1 < steps:
            tiles.append(wt[HEAD_DIM:])
    return tiles


def _prep_kernel(p_ref, pv_ref, nx_ref, cos_ref, sin_ref, mup_ref, mun_ref, w0_ref, wup_ref, a0_ref, aup_ref,
                 gup_ref, kk_ref, ka_ref, rk_ref, bd_ref,
                 v_out, abar_out, rbar_out, ktil_out, btil_out, gend_out, g_out, bonus_out, rows_ref):
    j = pl.program_id(0)
    n_ctx = CTX_LEN // SCAN_CHUNK
    p = p_ref[...].astype(jnp.float32)
    seq_start = (j == 0) | (j == n_ctx)
    seq_end = (j == n_ctx - 1) | (j == pl.num_programs(0) - 1)
    prev = _shift_rows(p, BATCH, jnp.where(seq_start, 0.0, pv_ref[...].astype(jnp.float32)), down=True)
    nxt = _shift_rows(p, BATCH, jnp.where(seq_end, 0.0, nx_ref[...].astype(jnp.float32)), down=False)
    ps = p + mup_ref[...] * (prev - p) + mun_ref[...] * (nxt - p)

    cos = cos_ref[...]
    sin = sin_ref[...]
    r = _rope(ps[:, 0:D_RW], cos, sin)
    k = _rope(ps[:, D_RW:2 * D_RW], cos, sin)
    v = ps[:, 2 * D_RW:3 * D_RW]
    o3 = 3 * D_RW
    wd = ps[:, o3:o3 + 2 * LORA_W]
    ad = ps[:, o3 + 2 * LORA_W:o3 + 2 * LORA_W + 2 * LORA_A]
    gd = ps[:, o3 + 2 * LORA_W + 2 * LORA_A:]

    w_pre = w0_ref[...] + jnp.dot(jnp.tanh(wd).astype(jnp.bfloat16), wup_ref[...], preferred_element_type=jnp.float32)
    decay = jnp.exp(-DECAY_SCALE * _sigmoid(w_pre))
    a = _sigmoid(a0_ref[...] + jnp.dot(ad.astype(jnp.bfloat16), aup_ref[...], preferred_element_type=jnp.float32))
    g = jnp.dot(_sigmoid(gd).astype(jnp.bfloat16), gup_ref[...], preferred_element_type=jnp.float32)

    bd = bd_ref[...]
    kk = k * kk_ref[...]
    kk = kk / jnp.maximum(jnp.sqrt(_group_sum(kk * kk, bd)), 1e-12)
    bonus = _group_sum(r * k * rk_ref[...], bd) * v
    ka = ka_ref[...]

    for out, val in ((g_out, g), (bonus_out, bonus)):
        for cb in range(D_RW // LANES):
            rows_ref[cb] = val[:, cb * LANES:(cb + 1) * LANES]
        for b in range(BATCH):
            for cb in range(D_RW // LANES):
                out[b, :, cb * LANES:(cb + 1) * LANES] = rows_ref[cb, pl.ds(b, SCAN_CHUNK, stride=BATCH), :]

    for t, tile in enumerate(_scan_tiles(v)):
        v_out[t] = tile
    for d in range(2):
        ds_ = slice(d * D_RW, (d + 1) * D_RW)
        a_d = a[:, ds_]
        gam, gam_excl = _chunk_cumprod(decay[:, ds_], reverse=(d == 1))
        inv = 1.0 / gam
        streams = ((abar_out, gam_excl * kk), (rbar_out, gam * r),
                   (ktil_out, k * (1.0 + (a_d - 1.0) * ka) * inv), (btil_out, kk * a_d * inv))
        for out, val in streams:
            for t, tile in enumerate(_scan_tiles(val)):
                out[d, t] = tile
        gam_last = gam[:BATCH] if d == 1 else gam[(SCAN_CHUNK - 1) * BATCH:]
        gend_out[d, 0] = _scan_tiles(gam_last)[0]


def _prep_call(p_rw_t, cos_t, sin_t, mu_prev, mu_next, w0, wup2, a0, aup2, gup, k_k, k_a, r_k, bd):
    rows = SCAN_CHUNK * BATCH
    n_chunks = T_ALL // SCAN_CHUNK
    full = lambda shape: pl.BlockSpec(shape, lambda j: (0,) * len(shape))
    tok = lambda width: pl.BlockSpec((rows, width), lambda j: (j, 0))
    perdir = pl.BlockSpec((2, SCAN_CHUNK, HEAD_DIM, LANES), lambda j: (0, j, 0, 0))
    two = jax.ShapeDtypeStruct((2, T_ALL, HEAD_DIM, LANES), jnp.float32)
    nat = jax.ShapeDtypeStruct((BATCH, SEQ, D_RW), jnp.float32)
    n_ctx = CTX_LEN // SCAN_CHUNK
    lat = pl.BlockSpec((BATCH, SCAN_CHUNK, D_RW), lambda j: (0, jnp.maximum(j - n_ctx, 0), 0))
    return pl.pallas_call(
        _prep_kernel,
        out_shape=(jax.ShapeDtypeStruct((T_ALL, HEAD_DIM, LANES), jnp.float32), two, two, two, two,
                   jax.ShapeDtypeStruct((2, n_chunks, HEAD_DIM, LANES), jnp.float32), nat, nat),
        grid=(n_chunks,),
        in_specs=[tok(D_RW_IN),
                  pl.BlockSpec((BATCH, D_RW_IN), lambda j: (jnp.maximum(j * SCAN_CHUNK - 1, 0), 0)),
                  pl.BlockSpec((BATCH, D_RW_IN), lambda j: (jnp.minimum((j + 1) * SCAN_CHUNK, T_ALL - 1), 0)),
                  tok(LANES), tok(LANES),
                  full((1, D_RW_IN)), full((1, D_RW_IN)),
                  full((1, 2 * D_RW)), full((2 * LORA_W, 2 * D_RW)),
                  full((1, 2 * D_RW)), full((2 * LORA_A, 2 * D_RW)),
                  full((LORA_G, D_RW)),
                  full((1, D_RW)), full((1, D_RW)), full((1, D_RW)),
                  full((D_RW, D_RW))],
        out_specs=(pl.BlockSpec((SCAN_CHUNK, HEAD_DIM, LANES), lambda j: (j, 0, 0)), perdir, perdir, perdir, perdir,
                   pl.BlockSpec((2, 1, HEAD_DIM, LANES), lambda j: (0, j, 0, 0)), lat, lat),
        scratch_shapes=[pltpu.VMEM((D_RW // LANES, SCAN_CHUNK * BATCH, LANES), jnp.float32)],
        compiler_params=pltpu.CompilerParams(dimension_semantics=("arbitrary",), vmem_limit_bytes=VMEM_LIMIT),
        name="prep",
    )(p_rw_t, p_rw_t, p_rw_t, cos_t, sin_t, mu_prev, mu_next, w0, wup2, a0, aup2, gup, k_k, k_a, r_k, bd)


def _rope_tables():
    nf = HEAD_DIM // 4
    pos = np.arange(SEQ)
    inv = ROPE_THETA ** (-np.arange(nf, dtype=np.float32) / nf)
    lane = np.arange(LANES) % HEAD_DIM
    half, pair, f = lane // 32, (lane % 32) // 16, lane % 16
    coord = np.where(half[None, :] == 0, (pos // GRID_W)[:, None], (pos % GRID_W)[:, None]).astype(np.float32)
    ang = coord * inv[f][None, :].astype(np.float32)
    cos = np.cos(ang).astype(np.float32)
    sin = np.sin(ang).astype(np.float32) * np.where(pair == 0, -1.0, 1.0)[None, :].astype(np.float32)
    cos = np.concatenate([np.ones((CTX_LEN, LANES), np.float32), cos], axis=0)
    sin = np.concatenate([np.zeros((CTX_LEN, LANES), np.float32), sin], axis=0)
    return jnp.asarray(np.repeat(cos, BATCH, axis=0)), jnp.asarray(np.repeat(sin, BATCH, axis=0))


def _scan_kernel(v_ref, abar_ref, rbar_ref, ktil_ref, btil_ref, gend_ref, o_ref, s_ref):
    d = pl.program_id(0)
    i = pl.program_id(1)

    @pl.when(i == 0)
    def _():
        s_ref[...] = jnp.zeros_like(s_ref)

    k_unroll = 32
    n_kb = HEAD_DIM // k_unroll
    zero = jnp.zeros((HEAD_DIM, LANES), jnp.float32)

    def make_step(with_out):
        def step(s, carry):
            t = jnp.where(d == 0, s, SCAN_CHUNK - 1 - s)
            vv = v_ref[t]

            def sk_body(kb, sk):
                for u in range(k_unroll):
                    k = kb * k_unroll + u
                    sk = sk + s_ref[k] * abar_ref[0, t, pl.ds(k, 1), :]
                return sk

            sk = lax.fori_loop(0, n_kb, sk_body, zero)

            def upd_body(kb, o):
                for u in range(k_unroll):
                    k = kb * k_unroll + u
                    s_new = s_ref[k] + (vv * ktil_ref[0, t, pl.ds(k, 1), :] - sk * btil_ref[0, t, pl.ds(k, 1), :])
                    s_ref[k] = s_new
                    if with_out:
                        o = o + s_new * rbar_ref[0, t, pl.ds(k, 1), :]
                return o

            o = lax.fori_loop(0, n_kb, upd_body, zero)
            if with_out:
                o_ref[0, t] = o
            return carry
        return step

    @pl.when(i < CTX_LEN // SCAN_CHUNK)
    def _():
        lax.fori_loop(0, SCAN_CHUNK, make_step(False), 0)

    @pl.when(i >= CTX_LEN // SCAN_CHUNK)
    def _():
        lax.fori_loop(0, SCAN_CHUNK, make_step(True), 0)

    def renorm(kb, carry):
        for u in range(k_unroll):
            k = kb * k_unroll + u
            s_ref[k] = s_ref[k] * gend_ref[0, 0, pl.ds(k, 1), :]
        return carry

    lax.fori_loop(0, n_kb, renorm, 0)


def _scan_call(v_t, abar_t, rbar_t, ktil_t, btil_t, gend_t):
    nc = CTX_LEN // SCAN_CHUNK
    nl = SEQ // SCAN_CHUNK

    def blk(d, i):
        rev = jnp.where(i < nc, nc - 1 - i, 2 * nc + nl - 1 - i)
        return jnp.where(d == 0, i, rev)

    def oblk(d, i):
        fwd = jnp.maximum(i - nc, 0)
        rev = jnp.where(i < nc, nl - 1, nc + nl - 1 - i)
        return jnp.where(d == 0, fwd, rev)

    shared = pl.BlockSpec((SCAN_CHUNK, HEAD_DIM, LANES), lambda d, i: (blk(d, i), 0, 0))
    perdir = pl.BlockSpec((1, SCAN_CHUNK, HEAD_DIM, LANES), lambda d, i: (d, blk(d, i), 0, 0))
    perchunk = pl.BlockSpec((1, 1, HEAD_DIM, LANES), lambda d, i: (d, blk(d, i), 0, 0))
    return pl.pallas_call(
        _scan_kernel,
        out_shape=jax.ShapeDtypeStruct((2, SEQ, HEAD_DIM, LANES), jnp.float32),
        grid=(2, nc + nl),
        in_specs=[shared, perdir, perdir, perdir, perdir, perchunk],
        out_specs=pl.BlockSpec((1, SCAN_CHUNK, HEAD_DIM, LANES), lambda d, i: (d, oblk(d, i), 0, 0)),
        scratch_shapes=[pltpu.VMEM((HEAD_DIM, HEAD_DIM, LANES), jnp.float32)],
        compiler_params=pltpu.CompilerParams(dimension_semantics=("arbitrary", "arbitrary"),
                                             vmem_limit_bytes=VMEM_LIMIT),
        name="scan",
    )(v_t, abar_t, rbar_t, ktil_t, btil_t, gend_t)


def _from_scan_layout(o):
    return o.reshape(SEQ, HEAD_DIM, H_RW, BATCH).transpose(3, 0, 2, 1).reshape(BATCH, SEQ, D_RW)


def _finish_kernel(na_ref, o_ref, bonus_ref, g_ref, x_ref, mod_ref, lng_ref, lnb_ref, wout_ref, n2g_ref, bd_ref,
                   wr_ref, br_ref, uj_ref, x1_out, h2_out, eid_out, wt_out, rank_out, cnt_out, run_ref):
    @pl.when((pl.program_id(0) == 0) & (pl.program_id(1) == 0))
    def _():
        run_ref[...] = jnp.zeros_like(run_ref)

    bd = bd_ref[...]
    o = o_ref[0]
    mu = _group_sum(o, bd) * (1.0 / HEAD_DIM)
    oc = o - mu
    var = _group_sum(oc * oc, bd) * (1.0 / HEAD_DIM)
    y = oc * lax.rsqrt(var + RW_LN_EPS) * lng_ref[...] + lnb_ref[...]
    rw = (y + bonus_ref[0]) * g_ref[0]
    mix = jnp.concatenate([na_ref[0], rw], axis=-1).astype(jnp.bfloat16)
    yx = jnp.dot(mix, wout_ref[...], preferred_element_type=jnp.float32)
    mod = mod_ref[0]
    x1 = x_ref[0] + mod[2:3, :] * yx
    x1_out[0] = x1
    h2 = _rmsnorm_rows(x1, n2g_ref[...]) * (1.0 + mod[4:5, :]) + mod[3:4, :]
    h2_out[0] = h2.astype(jnp.bfloat16)

    lg = lax.dot_general(wr_ref[...], h2, (((1,), (1,)), ((), ())), precision=_HI,
                         preferred_element_type=jnp.float32) + br_ref[...]
    gl = lg[0:MOE_GROUPS, :]
    grow = lax.broadcasted_iota(jnp.int32, gl.shape, 0)
    gmax = jnp.max(gl, axis=0, keepdims=True)
    g_sel = jnp.min(jnp.where(gl == gmax, grow, MOE_GROUPS), axis=0, keepdims=True)
    p_group = 1.0 / jnp.sum(jnp.exp(gl - gmax), axis=0, keepdims=True)
    el = jnp.zeros((MOE_PER_GROUP, FIN_TILE), jnp.float32)
    for gi in range(MOE_GROUPS):
        el = jnp.where(g_sel == gi, lg[8 + gi * MOE_PER_GROUP:8 + (gi + 1) * MOE_PER_GROUP, :], el)
    erow = lax.broadcasted_iota(jnp.int32, el.shape, 0)
    m1 = jnp.max(el, axis=0, keepdims=True)
    i1 = jnp.min(jnp.where(el == m1, erow, MOE_PER_GROUP), axis=0, keepdims=True)
    el2 = jnp.where(erow == i1, -jnp.inf, el)
    m2 = jnp.max(el2, axis=0, keepdims=True)
    i2 = jnp.min(jnp.where(el2 == m2, erow, MOE_PER_GROUP), axis=0, keepdims=True)
    e21 = jnp.exp(m2 - m1)
    e0 = g_sel * MOE_PER_GROUP + i1
    e1 = g_sel * MOE_PER_GROUP + i2
    eid_out[0:1, :] = e0
    eid_out[1:2, :] = e1
    wt_out[0:1, :] = p_group / (1.0 + e21)
    wt_out[1:2, :] = p_group * e21 / (1.0 + e21)

    xrow = lax.broadcasted_iota(jnp.int32, (N_EXPERTS, FIN_TILE), 0)
    m0 = xrow == e0
    m1b = xrow == e1
    c0 = jnp.dot(jnp.where(m0, 1.0, 0.0).astype(jnp.bfloat16), uj_ref[...], preferred_element_type=jnp.float32)
    c1 = jnp.dot(jnp.where(m1b, 1.0, 0.0).astype(jnp.bfloat16), uj_ref[...], preferred_element_type=jnp.float32)
    run = run_ref[...]
    tot0 = c0[:, FIN_TILE:]
    rank_out[0:1, :] = jnp.sum(jnp.where(m0, run + c0[:, :FIN_TILE], 0.0), axis=0, keepdims=True).astype(jnp.int32)
    rank_out[1:2, :] = jnp.sum(jnp.where(m1b, run + tot0 + c1[:, :FIN_TILE], 0.0), axis=0,
                               keepdims=True).astype(jnp.int32)
    run = run + tot0 + c1[:, FIN_TILE:]
    run_ref[...] = run
    cnt_out[...] = run

def _finish_call(na_x, o_rw, bonus, g, x, mod3, ln_g, ln_b, w_out_bf16, norm2_g, bd, wr_t, br, uj):
    n_lat = SEQ // FIN_TILE
    full = lambda shape: pl.BlockSpec(shape, lambda b, j: (0,) * len(shape))
    lat = lambda width: pl.BlockSpec((1, FIN_TILE, width), lambda b, j: (b, j, 0))
    tokl = pl.BlockSpec((2, FIN_TILE), lambda b, j: (0, b * n_lat + j))
    return pl.pallas_call(
        _finish_kernel,
        out_shape=(jax.ShapeDtypeStruct((BATCH, SEQ, D_MODEL), jnp.float32),
                   jax.ShapeDtypeStruct((BATCH, SEQ, D_MODEL), jnp.bfloat16),
                   jax.ShapeDtypeStruct((2, N_TOK), jnp.int32),
                   jax.ShapeDtypeStruct((2, N_TOK), jnp.float32),
                   jax.ShapeDtypeStruct((2, N_TOK), jnp.int32),
                   jax.ShapeDtypeStruct((N_EXPERTS, FIN_TILE), jnp.float32)),
        grid=(BATCH, n_lat),
        in_specs=[lat(D_NA), lat(D_RW), lat(D_RW), lat(D_RW), lat(D_MODEL),
                  pl.BlockSpec((1, 6, D_MODEL), lambda b, j: (b, 0, 0)),
                  full((1, D_RW)), full((1, D_RW)), full((D_MODEL, D_MODEL)), full((1, D_MODEL)),
                  full((D_RW, D_RW)), full((LOGIT_ROWS, D_MODEL)), full((LOGIT_ROWS, 1)),
                  full((FIN_TILE, 2 * FIN_TILE))],
        out_specs=(lat(D_MODEL), lat(D_MODEL), tokl, tokl, tokl, full((N_EXPERTS, FIN_TILE))),
        scratch_shapes=[pltpu.VMEM((N_EXPERTS, FIN_TILE), jnp.float32)],
        compiler_params=pltpu.CompilerParams(dimension_semantics=("arbitrary", "arbitrary"),
                                             vmem_limit_bytes=VMEM_LIMIT),
        name="finish",
    )(na_x, o_rw, bonus, g, x, mod3, ln_g, ln_b, w_out_bf16, norm2_g, bd, wr_t, br, uj)


def _moe_kernel(te_ref, tv_ref, tx_ref, xs_ref, w1_ref, w3_ref, w2_ref, y_ref, w1s, w3s, w2s):
    t = pl.program_id(0)

    @pl.when((t == 0) | (te_ref[t] != te_ref[jnp.maximum(t - 1, 0)]))
    def _():
        w1s[...] = w1_ref[0].astype(jnp.bfloat16)
        w3s[...] = w3_ref[0].astype(jnp.bfloat16)
        w2s[...] = w2_ref[0].astype(jnp.bfloat16)

    @pl.when(tv_ref[t] == 1)
    def _():
        xs = xs_ref[...]
        h1 = jnp.dot(xs, w1s[...], preferred_element_type=jnp.float32)
        h3 = jnp.dot(xs, w3s[...], preferred_element_type=jnp.float32)
        he = (_silu(h1) * h3).astype(jnp.bfloat16)
        y_ref[...] = jnp.dot(he, w2s[...], preferred_element_type=jnp.float32).astype(jnp.bfloat16)

    @pl.when(tv_ref[t] == 0)
    def _():
        y_ref[...] = jnp.zeros_like(y_ref)


def _moe_call(tile_e, tile_valid, tile_src, xs, w1, w3, w2):
    return pl.pallas_call(
        _moe_kernel,
        out_shape=jax.ShapeDtypeStruct((N_SLOTS, D_MODEL), jnp.bfloat16),
        grid_spec=pltpu.PrefetchScalarGridSpec(
            num_scalar_prefetch=3,
            grid=(N_MOE_TILES,),
            in_specs=[pl.BlockSpec((MOE_TILE, D_MODEL), lambda t, te, tv, tx: (tx[t], 0)),
                      pl.BlockSpec((1, D_MODEL, D_EXPERT), lambda t, te, tv, tx: (te[t], 0, 0)),
                      pl.BlockSpec((1, D_MODEL, D_EXPERT), lambda t, te, tv, tx: (te[t], 0, 0)),
                      pl.BlockSpec((1, D_EXPERT, D_MODEL), lambda t, te, tv, tx: (te[t], 0, 0))],
            out_specs=pl.BlockSpec((MOE_TILE, D_MODEL), lambda t, te, tv, tx: (t, 0)),
            scratch_shapes=[pltpu.VMEM((D_MODEL, D_EXPERT), jnp.bfloat16),
                            pltpu.VMEM((D_MODEL, D_EXPERT), jnp.bfloat16),
                            pltpu.VMEM((D_EXPERT, D_MODEL), jnp.bfloat16)]),
        compiler_params=pltpu.CompilerParams(dimension_semantics=("arbitrary",), vmem_limit_bytes=VMEM_LIMIT),
        name="moe",
    )(tile_e, tile_valid, tile_src, xs, w1, w3, w2)


def _moe_plan(eid, rank, cnt):
    counts = cnt[:, 0].astype(jnp.int32)
    gsz = ((counts + MOE_TILE - 1) // MOE_TILE) * MOE_TILE
    gend = jnp.cumsum(gsz)
    goff = gend - gsz
    pos = rank
    for e in range(N_EXPERTS):
        pos = pos + jnp.where(eid == e, goff[e], 0)
    starts = jnp.arange(N_MOE_TILES, dtype=jnp.int32) * MOE_TILE
    te = jnp.sum((gend[None, :] <= starts[:, None]).astype(jnp.int32), axis=1)
    valid = (te < N_EXPERTS).astype(jnp.int32)
    last = jnp.max(jnp.where(counts > 0, jnp.arange(N_EXPERTS), 0)).astype(jnp.int32)
    te = jnp.where(valid == 1, te, last)
    n_valid = gend[-1] // MOE_TILE
    tx = jnp.clip(jnp.arange(N_MOE_TILES, dtype=jnp.int32), 0, jnp.maximum(n_valid - 1, 0))
    tok = jnp.broadcast_to(jnp.arange(N_TOK, dtype=jnp.int32), (2, N_TOK))
    src_tok = jnp.zeros((N_SLOTS,), jnp.int32).at[pos.reshape(-1)].set(tok.reshape(-1))
    return pos, src_tok, te, valid, tx


def _final_kernel(x1_ref, ya_ref, yb_ref, wt_ref, mod_ref, o_ref):
    w = wt_ref[...]
    ya = ya_ref[0].astype(jnp.float32)
    yb = yb_ref[0].astype(jnp.float32)
    o_ref[0] = x1_ref[0] + mod_ref[0][5:6, :] * (w[:, 0:1] * ya + w[:, 1:2] * yb)


def _final_call(x1, ya, yb, wt_rows, mod3):
    n_lat = SEQ // TOK_TILE
    lat = pl.BlockSpec((1, TOK_TILE, D_MODEL), lambda b, j: (b, j, 0))
    return pl.pallas_call(
        _final_kernel,
        out_shape=jax.ShapeDtypeStruct((BATCH, SEQ, D_MODEL), jnp.float32),
        grid=(BATCH, n_lat),
        in_specs=[lat, lat, lat, pl.BlockSpec((TOK_TILE, 2), lambda b, j: (b * n_lat + j, 0)),
                  pl.BlockSpec((1, 6, D_MODEL), lambda b, j: (b, 0, 0))],
        out_specs=lat,
        name="final",
    )(x1, ya, yb, wt_rows, mod3)


def _block_diag2(w):
    z = jnp.zeros_like(w[0])
    return jnp.concatenate([jnp.concatenate([w[0], z], axis=1), jnp.concatenate([z, w[1]], axis=1)], axis=0)


def kernel(x, c, ctx, c_ctx, w_mod, b_mod, norm1_g, norm2_g, w_in, na_q_g, na_k_g, na_rpb, rw_mu_prev, rw_mu_next,
           rw_w0, rw_w_up, rw_a0, rw_a_up, rw_g_up, rw_k_k, rw_k_a, rw_r_k, rw_ln_g, rw_ln_b, w_out, moe_wg, moe_bg,
           moe_we, moe_be, moe_w1, moe_w3, moe_w2):
    bf = jnp.bfloat16
    mod_rows = BATCH + 8
    cs = jnp.concatenate([c, c_ctx[None, :], jnp.zeros((mod_rows - BATCH - 1, D_MODEL), jnp.float32)], axis=0)
    mod = _mod_call(cs, w_mod[0], b_mod[0][None, :])
    mod3 = mod.reshape(mod_rows, 6, D_MODEL)

    head = jnp.arange(D_RW) // HEAD_DIM
    bd = (head[:, None] == head[None, :]).astype(bf)
    qkv, p_rw = _inproj_call(x, ctx, mod3, norm1_g[0][None, :], w_in[0].astype(bf),
                             jnp.tile(na_q_g[0], H_NA)[None, :], jnp.tile(na_k_g[0], H_NA)[None, :], bd)

    na_x = _na_call(qkv, _na_bias_table(na_rpb[0]))

    cos_t, sin_t = _rope_tables()
    p_rw_t = p_rw.transpose(1, 0, 2).reshape(T_ALL * BATCH, D_RW_IN)
    v, abar, rbar, ktil, btil, gend, g, bonus = _prep_call(
        p_rw_t, cos_t, sin_t, rw_mu_prev[0][None, :], rw_mu_next[0][None, :],
        rw_w0[0].reshape(1, 2 * D_RW), _block_diag2(rw_w_up[0]).astype(bf),
        rw_a0[0].reshape(1, 2 * D_RW), _block_diag2(rw_a_up[0]).astype(bf),
        rw_g_up[0].astype(bf), rw_k_k[0][None, :], rw_k_a[0][None, :], rw_r_k[0].reshape(1, D_RW), bd)

    o2 = _scan_call(v, abar, rbar, ktil, btil, gend)
    o_rw = _from_scan_layout(o2[0] + o2[1])

    wr_t = jnp.zeros((LOGIT_ROWS, D_MODEL), jnp.float32)
    wr_t = wr_t.at[0:MOE_GROUPS].set(moe_wg[0].T)
    wr_t = wr_t.at[8:].set(moe_we[0].transpose(0, 2, 1).reshape(N_EXPERTS, D_MODEL))
    br = jnp.zeros((LOGIT_ROWS,), jnp.float32).at[0:MOE_GROUPS].set(moe_bg[0]).at[8:].set(moe_be[0].reshape(-1))
    tri = jnp.arange(FIN_TILE)
    uj = jnp.concatenate([(tri[:, None] < tri[None, :]).astype(bf), jnp.ones((FIN_TILE, FIN_TILE), bf)], axis=1)
    x1, h2p, eid, wts, rank, cnt = _finish_call(na_x, o_rw, bonus, g, x, mod3, rw_ln_g[0][None, :],
                                                rw_ln_b[0][None, :], w_out[0].astype(bf), norm2_g[0][None, :], bd,
                                                wr_t, br[:, None], uj)

    pos, src_tok, tile_e, tile_valid, tile_src = _moe_plan(eid, rank, cnt)
    xs = jnp.take(h2p.reshape(N_TOK, D_MODEL), src_tok, axis=0, mode="clip")
    ys = _moe_call(tile_e, tile_valid, tile_src, xs, moe_w1[0], moe_w3[0], moe_w2[0])
    ya = jnp.take(ys, pos[0], axis=0, mode="clip").reshape(BATCH, SEQ, D_MODEL)
    yb = jnp.take(ys, pos[1], axis=0, mode="clip").reshape(BATCH, SEQ, D_MODEL)
    return _final_call(x1, ya, yb, wts.T, mod3)
```
